```python
import math
import jax, jax.numpy as jnp
from jax import lax
import numpy as np

D_MODEL = 1024
BATCH = 2
SEQ = 8192
DEPTH = 1

MEM_LEN = 256
BLOCK_Q = 128
SB_HEADS = 8
SB_HEAD_DIM = 64
DF_HEADS = 4
DF_HEAD_DIM = 64
MX_HEADS = 4
MX_HEAD_DIM = 128
SB_WIDTH = SB_HEADS * SB_HEAD_DIM
DF_WIDTH = DF_HEADS * 2 * DF_HEAD_DIM
MX_WIDTH = MX_HEADS * MX_HEAD_DIM
IN_WIDTH = 3 * SB_WIDTH + 3 * DF_WIDTH + MX_WIDTH
N_BRANCH = 3
ROPE_THETA = 500000.0
ROPE_DIM = DF_HEAD_DIM // 4
PEER_HEADS = 8
PEER_NKEYS = 128
PEER_EXPERTS = PEER_NKEYS * PEER_NKEYS
PEER_TOPK = 16
PEER_QDIM = 256
PEER_HALF = PEER_QDIM // 2
PEER_CHUNK = 128
LN_EPS = 1e-5
RMS_EPS = 1e-5
DEEPNORM_ALPHA = (2.0 * DEPTH) ** 0.25
DEEPNORM_BETA = (8.0 * DEPTH) ** -0.25

kernel_name = "hybrid_sb_diff_mem_peer_block"


def layer_norm(x, g, b):
    xf = x.astype(jnp.float32)
    mu = jnp.mean(xf, axis=-1, keepdims=True)
    var = jnp.mean(jnp.square(xf - mu), axis=-1, keepdims=True)
    return ((xf - mu) * lax.rsqrt(var + LN_EPS) * g.astype(jnp.float32) + b.astype(jnp.float32)).astype(x.dtype)


def rope_tables(positions):
    inv = ROPE_THETA ** (-jnp.arange(0, ROPE_DIM, 2, dtype=jnp.float32) / ROPE_DIM)
    ang = positions.astype(jnp.float32)[..., None] * inv
    return jnp.cos(ang), jnp.sin(ang)


def apply_partial_rotary(x, cos, sin):
    xr, xp = x[..., :ROPE_DIM], x[..., ROPE_DIM:]
    x1, x2 = xr[..., :ROPE_DIM // 2], xr[..., ROPE_DIM // 2:]
    rot = jnp.concatenate([x1 * cos - x2 * sin, x2 * cos + x1 * sin], axis=-1).astype(x.dtype)
    return jnp.concatenate([rot, xp], axis=-1)


def stick_breaking_attention(q, k, v):
    B, H, S, d = q.shape
    nb = S // BLOCK_Q
    scale = d ** -0.5
    key_pos = jnp.arange(S)
    qb = q.reshape(B, H, nb, BLOCK_Q, d).transpose(2, 0, 1, 3, 4)

    def block(args):
        q_blk, i = args
        qpos = i * BLOCK_Q + jnp.arange(BLOCK_Q)
        z = jnp.einsum('bhqd,bhkd->bhqk', q_blk, k, preferred_element_type=jnp.float32) * scale
        causal = key_pos[None, :] < qpos[:, None]
        log_beta = jax.nn.log_sigmoid(z)
        log_one_minus = jnp.where(causal, jax.nn.log_sigmoid(-z), 0.0)
        rest = lax.cumsum(log_one_minus, axis=log_one_minus.ndim - 1, reverse=True) - log_one_minus
        a = jnp.where(causal, jnp.exp(log_beta + rest), 0.0)
        return jnp.einsum('bhqk,bhkd->bhqd', a.astype(v.dtype), v)

    out = lax.map(block, (qb, jnp.arange(nb)))
    return out.transpose(1, 2, 0, 3, 4).reshape(B, H, S, d)


def differential_attention(q, k, v, lam):
    B, H, _, S, d = q.shape
    nb = S // BLOCK_Q
    scale = d ** -0.5
    key_pos = jnp.arange(S)
    qb = q.reshape(B, H, 2, nb, BLOCK_Q, d).transpose(3, 0, 1, 2, 4, 5)

    def block(args):
        q_blk, i = args
        qpos = i * BLOCK_Q + jnp.arange(BLOCK_Q)
        s = jnp.einsum('bhmqd,bhmkd->bhmqk', q_blk, k, preferred_element_type=jnp.float32) * scale
        s = jnp.where(key_pos[None, :] <= qpos[:, None], s, -jnp.inf)
        p = jax.nn.softmax(s, axis=-1)
        attn = p[:, :, 0] - lam * p[:, :, 1]
        return jnp.einsum('bhqk,bhkd->bhqd', attn.astype(v.dtype), v)

    out = lax.map(block, (qb, jnp.arange(nb)))
    return out.transpose(1, 2, 0, 3, 4).reshape(B, H, S, 2 * d)


def memory_cross_attention(q, k, v):
    scale = q.shape[-1] ** -0.5
    s = jnp.einsum('bshd,bmhd->bhsm', q, k, preferred_element_type=jnp.float32) * scale
    p = jax.nn.softmax(s, axis=-1)
    return jnp.einsum('bhsm,bmhd->bshd', p.astype(v.dtype), v)


def peer_ffn(x, w_query, sub_keys, u_tab, v_tab):
    B, S, D = x.shape
    nc = S // PEER_CHUNK
    xc = x.reshape(B, nc, PEER_CHUNK, D).transpose(1, 0, 2, 3)

    def chunk(xb):
        C = xb.shape[1]
        q = (xb @ w_query).reshape(B, C, PEER_HEADS, 2, PEER_HALF)
        s = jnp.einsum('bchpd,hpnd->bchpn', q, sub_keys, preferred_element_type=jnp.float32)
        top_s, top_i = lax.top_k(s, PEER_TOPK)
        cand_s = top_s[..., 0, :, None] + top_s[..., 1, None, :]
        cand_i = top_i[..., 0, :, None] * PEER_NKEYS + top_i[..., 1, None, :]
        cand_s = cand_s.reshape(B, C, PEER_HEADS, PEER_TOPK * PEER_TOPK)
        cand_i = cand_i.reshape(B, C, PEER_HEADS, PEER_TOPK * PEER_TOPK)
        best_s, best_pos = lax.top_k(cand_s, PEER_TOPK)
        ids = jnp.take_along_axis(cand_i, best_pos, axis=-1)
        g = jax.nn.softmax(best_s, axis=-1)
        ids = ids.reshape(B, C, PEER_HEADS * PEER_TOPK)
        g = g.reshape(B, C, PEER_HEADS * PEER_TOPK)
        u = jnp.take(u_tab, ids, axis=0)
        v = jnp.take(v_tab, ids, axis=0)
        h = jnp.einsum('bcd,bced->bce', xb, u, preferred_element_type=jnp.float32)
        a = jax.nn.gelu(h, approximate=False) * g
        return jnp.einsum('bce,bced->bcd', a.astype(v.dtype), v)

    out = lax.map(chunk, xc)
    return out.transpose(1, 0, 2, 3).reshape(B, S, D)


def setup_inputs(seed: int = 0) -> dict:
    key = jax.random.key(seed)
    ks = jax.random.split(key, 24)
    f32 = jnp.float32
    nrm = lambda k, shape, s: jax.random.normal(k, shape, f32) * s
    D = D_MODEL
    x = nrm(ks[0], (BATCH, SEQ, D), 1.0)
    mem = nrm(ks[1], (BATCH, MEM_LEN, D), 1.0)
    offset = jax.random.randint(ks[2], (BATCH, 1), 0, 1024, dtype=jnp.int32)
    positions = offset + jnp.arange(SEQ, dtype=jnp.int32)[None, :]
    return {
        "x": x,
        "mem": mem,
        "positions": positions,
        "w_in": nrm(ks[3], (DEPTH, D, IN_WIDTH), D ** -0.5),
        "w_mem_kv": nrm(ks[4], (DEPTH, D, 2 * MX_WIDTH), D ** -0.5),
        "df_lambda": nrm(ks[5], (DEPTH, 4, DF_HEAD_DIM), 0.1),
        "df_norm_g": 1.0 + nrm(ks[6], (DEPTH, 2 * DF_HEAD_DIM), 0.02),
        "w_gate": nrm(ks[7], (DEPTH, D, N_BRANCH * D), D ** -0.5),
        "b_gate": nrm(ks[8], (DEPTH, N_BRANCH * D), 0.02),
        "w_br_sb": nrm(ks[9], (DEPTH, SB_WIDTH, D), SB_WIDTH ** -0.5),
        "w_br_df": nrm(ks[10], (DEPTH, DF_WIDTH, D), DF_WIDTH ** -0.5),
        "w_br_mx": nrm(ks[11], (DEPTH, MX_WIDTH, D), MX_WIDTH ** -0.5),
        "w_out": nrm(ks[12], (DEPTH, D, D), DEEPNORM_BETA * D ** -0.5),
        "ln1_g": 1.0 + nrm(ks[13], (DEPTH, D), 0.02),
        "ln1_b": nrm(ks[14], (DEPTH, D), 0.02),
        "w_query": nrm(ks[15], (DEPTH, D, PEER_HEADS * PEER_QDIM), D ** -0.5),
        "sub_keys": nrm(ks[16], (DEPTH, PEER_HEADS, 2, PEER_NKEYS, PEER_HALF), PEER_HALF ** -0.5),
        "u_tab": nrm(ks[17], (DEPTH, PEER_EXPERTS, D), D ** -0.5),
        "v_tab": nrm(ks[18], (DEPTH, PEER_EXPERTS, D), DEEPNORM_BETA),
        "ln2_g": 1.0 + nrm(ks[19], (DEPTH, D), 0.02),
        "ln2_b": nrm(ks[20], (DEPTH, D), 0.02),
    }


def reference(x, mem, positions, w_in, w_mem_kv, df_lambda, df_norm_g, w_gate, b_gate,
              w_br_sb, w_br_df, w_br_mx, w_out, ln1_g, ln1_b, w_query, sub_keys,
              u_tab, v_tab, ln2_g, ln2_b):
    B, S, D = x.shape
    M = mem.shape[1]
    cos, sin = rope_tables(positions)
    cos = cos[:, :, None, None, :]
    sin = sin[:, :, None, None, :]
    for l in range(DEPTH):
        lambda_init = 0.8 - 0.6 * math.exp(-0.3 * l)
        h = x
        proj = h @ w_in[l]
        sb_q, sb_k, sb_v, df_q, df_k, df_v, mx_q = jnp.split(
            proj, [SB_WIDTH, 2 * SB_WIDTH, 3 * SB_WIDTH,
                   3 * SB_WIDTH + DF_WIDTH, 3 * SB_WIDTH + 2 * DF_WIDTH,
                   3 * SB_WIDTH + 3 * DF_WIDTH], axis=-1)
        to_heads = lambda t: t.reshape(B, S, SB_HEADS, SB_HEAD_DIM).transpose(0, 2, 1, 3)
        o_sb = stick_breaking_attention(to_heads(sb_q), to_heads(sb_k), to_heads(sb_v))
        o_sb = o_sb.transpose(0, 2, 1, 3).reshape(B, S, SB_WIDTH)
        df_q = apply_partial_rotary(df_q.reshape(B, S, DF_HEADS, 2, DF_HEAD_DIM), cos, sin)
        df_k = apply_partial_rotary(df_k.reshape(B, S, DF_HEADS, 2, DF_HEAD_DIM), cos, sin)
        df_v = df_v.reshape(B, S, DF_HEADS, 2 * DF_HEAD_DIM).transpose(0, 2, 1, 3)
        lam_p = df_lambda[l].astype(jnp.float32)
        lam = (jnp.exp(jnp.sum(lam_p[0] * lam_p[1])) - jnp.exp(jnp.sum(lam_p[2] * lam_p[3]))
               + lambda_init)
        o_df = differential_attention(df_q.transpose(0, 2, 3, 1, 4), df_k.transpose(0, 2, 3, 1, 4),
                                      df_v, lam)
        o_df_f = o_df.astype(jnp.float32)
        o_df_f = o_df_f * lax.rsqrt(jnp.mean(jnp.square(o_df_f), axis=-1, keepdims=True) + RMS_EPS)
        o_df = (o_df_f * df_norm_g[l].astype(jnp.float32) * (1.0 - lambda_init)).astype(x.dtype)
        o_df = o_df.transpose(0, 2, 1, 3).reshape(B, S, DF_WIDTH)
        mem_k, mem_v = jnp.split(mem @ w_mem_kv[l], 2, axis=-1)
        o_mx = memory_cross_attention(mx_q.reshape(B, S, MX_HEADS, MX_HEAD_DIM),
                                      mem_k.reshape(B, M, MX_HEADS, MX_HEAD_DIM),
                                      mem_v.reshape(B, M, MX_HEADS, MX_HEAD_DIM))
        o_mx = o_mx.reshape(B, S, MX_WIDTH)
        gates = jax.nn.sigmoid(h @ w_gate[l] + b_gate[l]).reshape(B, S, N_BRANCH, D)
        merged = (gates[:, :, 0] * (o_sb @ w_br_sb[l])
                  + gates[:, :, 1] * (o_df @ w_br_df[l])
                  + gates[:, :, 2] * (o_mx @ w_br_mx[l]))
        mix = merged @ w_out[l]
        x = layer_norm(DEEPNORM_ALPHA * x + mix, ln1_g[l], ln1_b[l])
        ffn = peer_ffn(x, w_query[l], sub_keys[l], u_tab[l], v_tab[l])
        x = layer_norm(DEEPNORM_ALPHA * x + ffn, ln2_g[l], ln2_b[l])
    return x
```

```python
import functools
import math

import jax
import jax.numpy as jnp
from jax import lax
from jax.experimental import pallas as pl
from jax.experimental.pallas import tpu as pltpu

F32 = jnp.float32
BF16 = jnp.bfloat16

WIDTH = 512
N_GROUPS = 7
HEAD_COLS = 128
HALF_COLS = 64
ROPE_THETA = 500000.0
ROPE_DIM = 16
MX_HEAD_DIM = 128
PEER_HEADS = 8
PEER_NKEYS = 128
PEER_TOPK = 16
PEER_HALF = 128
LN_EPS = 1e-5
RMS_EPS = 1e-5
SB_SKIP_BELOW = -110.0

VMEM_LIMIT_BYTES = 56 * 1024 * 1024
LANES = 128


def _cparams(*sem):
    return pltpu.CompilerParams(dimension_semantics=sem, vmem_limit_bytes=VMEM_LIMIT_BYTES)


def _tile(n, want):
    t = min(n, want)
    while n % t:
        t //= 2
    return t


def _dot(a, b):
    return jnp.dot(a, b, preferred_element_type=F32)


def _dot_nt(a, b):
    return lax.dot_general(a, b, (((1,), (1,)), ((), ())), preferred_element_type=F32)


def _in_proj_body(x_ref, pos_ref, inv_ref, w_ref, sbq, sbk, sbv, dfq, dfk, dfv, mxq):
    xb = x_ref[...].astype(BF16)
    ang = pos_ref[...].astype(F32) * inv_ref[...]
    d = lax.broadcasted_iota(jnp.int32, ang.shape, 1) & (HALF_COLS - 1)
    first = d < ROPE_DIM // 2
    rot = d < ROPE_DIM
    cos = jnp.where(rot, jnp.cos(ang), 1.0)
    sin = jnp.sin(ang)
    sin = jnp.where(first, -sin, jnp.where(rot, sin, 0.0))

    def proj(g):
        return _dot(xb, w_ref[:, g * WIDTH:(g + 1) * WIDTH])

    def rotary(p):
        outs = []
        for j in range(WIDTH // LANES):
            xj = p[:, j * LANES:(j + 1) * LANES]
            partner = jnp.where(first, pltpu.roll(xj, LANES - ROPE_DIM // 2, 1),
                                pltpu.roll(xj, ROPE_DIM // 2, 1))
            outs.append(xj * cos + partner * sin)
        return jnp.concatenate(outs, axis=1)

    scale = HALF_COLS ** -0.5
    sbq[...] = (proj(0) * scale).astype(BF16)
    sbk[...] = proj(1).astype(BF16)
    sbv[...] = proj(2).astype(BF16)
    dfq[...] = (rotary(proj(3)) * scale).astype(BF16)
    dfk[...] = rotary(proj(4)).astype(BF16)
    dfv[...] = proj(5).astype(BF16)
    mxq[...] = proj(6).astype(BF16)


def _in_proj(x2, pos2, inv_lane, w_in):
    T, D = x2.shape
    tm = _tile(T, 512)
    out = jax.ShapeDtypeStruct((T, WIDTH), BF16)
    blk = pl.BlockSpec((tm, WIDTH), lambda i: (i, 0))
    return pl.pallas_call(
        _in_proj_body,
        grid=(T // tm,),
        in_specs=[pl.BlockSpec((tm, D), lambda i: (i, 0)),
                  pl.BlockSpec((tm, 1), lambda i: (i, 0)),
                  pl.BlockSpec((1, LANES), lambda i: (0, 0)),
                  pl.BlockSpec((D, N_GROUPS * WIDTH), lambda i: (0, 0))],
        out_specs=[blk] * N_GROUPS,
        out_shape=[out] * N_GROUPS,
        compiler_params=_cparams("parallel"),
        name="in_proj",
    )(x2, pos2, inv_lane, w_in)


def _sb_body(q_ref, k_ref, v_ref, o_ref, acc_ref, c_ref, *, tq):
    i = pl.program_id(2)
    qf = q_ref[...].astype(F32)
    lane = lax.broadcasted_iota(jnp.int32, qf.shape, 1)
    low = lane < HALF_COLS
    qs = (jnp.where(low, qf, 0.0).astype(BF16), jnp.where(low, 0.0, qf).astype(BF16))
    row = lax.broadcasted_iota(jnp.int32, (tq, tq), 0)
    col = lax.broadcasted_iota(jnp.int32, (tq, tq), 1)
    later = (row > col).astype(BF16)
    acc_ref[...] = jnp.zeros_like(acc_ref)
    c_ref[...] = jnp.zeros_like(c_ref)

    def cond(carry):
        kb, cmax = carry
        return jnp.logical_and(kb >= 0, cmax > SB_SKIP_BELOW)

    def body(carry):
        kb, _ = carry
        start = pl.multiple_of(kb * tq, tq)
        k = k_ref[pl.ds(start, tq), :]
        v = v_ref[pl.ds(start, tq), :]
        causal = (kb * tq + col) < (i * tq + row)
        cmax = None
        for h in range(2):
            z = _dot_nt(qs[h], k)
            t = jnp.log1p(jnp.exp(-jnp.abs(z)))
            log_beta = jnp.minimum(z, 0.0) - t
            lom = jnp.where(causal, -jnp.maximum(z, 0.0) - t, 0.0)
            hi = lom.astype(BF16)
            lo = (lom - hi.astype(F32)).astype(BF16)
            rest = _dot(hi, later) + _dot(lo, later)
            c = c_ref[h]
            a = jnp.where(causal, jnp.exp(log_beta + rest + c), 0.0)
            acc_ref[h] += _dot(a.astype(BF16), v)
            c_new = c + rest[:, 0:1] + lom[:, 0:1]
            c_ref[h] = c_new
            m = jnp.max(c_new)
            cmax = m if cmax is None else jnp.maximum(cmax, m)
        return kb - 1, cmax

    lax.while_loop(cond, body, (i, jnp.float32(0.0)))
    o_ref[...] = jnp.where(low, acc_ref[0], acc_ref[1]).astype(BF16)


def _sb_attention(q, k, v, B, S):
    T = q.shape[0]
    tq = _tile(S, 256)
    nq = S // tq
    k3 = k.reshape(B, S, WIDTH)
    v3 = v.reshape(B, S, WIDTH)
    kv_spec = pl.BlockSpec((None, S, HEAD_COLS), lambda b, p, i: (b, 0, p))
    return pl.pallas_call(
        functools.partial(_sb_body, tq=tq),
        grid=(B, WIDTH // HEAD_COLS, nq),
        in_specs=[pl.BlockSpec((tq, HEAD_COLS), lambda b, p, i: (b * nq + i, p)), kv_spec, kv_spec],
        out_specs=pl.BlockSpec((tq, HEAD_COLS), lambda b, p, i: (b * nq + i, p)),
        out_shape=jax.ShapeDtypeStruct((T, WIDTH), BF16),
        scratch_shapes=[pltpu.VMEM((2, tq, HEAD_COLS), F32), pltpu.VMEM((2, tq, 1), F32)],
        compiler_params=_cparams("parallel", "parallel", "arbitrary"),
        name="sb_attention",
    )(q, k3, v3)


def _df_body(lam_ref, g_ref, q_ref, k_ref, v_ref, o_ref, m_ref, l_ref, acc_ref, *, tq, lambda_init):
    i = pl.program_id(2)
    qf = q_ref[...].astype(F32)
    lane = lax.broadcasted_iota(jnp.int32, qf.shape, 1)
    low = lane < HALF_COLS
    qs = (jnp.where(low, qf, 0.0).astype(BF16), jnp.where(low, 0.0, qf).astype(BF16))
    m_ref[...] = jnp.full_like(m_ref, -jnp.inf)
    l_ref[...] = jnp.zeros_like(l_ref)
    acc_ref[...] = jnp.zeros_like(acc_ref)

    def step(kb, diagonal):
        start = pl.multiple_of(kb * tq, tq)
        k = k_ref[pl.ds(start, tq), :]
        v = v_ref[pl.ds(start, tq), :]
        for m in range(2):
            s = _dot_nt(qs[m], k)
            if diagonal:
                row = lax.broadcasted_iota(jnp.int32, (tq, tq), 0)
                col = lax.broadcasted_iota(jnp.int32, (tq, tq), 1)
                s = jnp.where(col <= row, s, -jnp.inf)
            m_old = m_ref[m]
            m_new = jnp.maximum(m_old, jnp.max(s, axis=1, keepdims=True))
            p = jnp.exp(s - m_new)
            alpha = jnp.exp(m_old - m_new)
            l_ref[m] = alpha * l_ref[m] + jnp.sum(p, axis=1, keepdims=True)
            acc_ref[m] = alpha * acc_ref[m] + _dot(p.astype(BF16), v)
            m_ref[m] = m_new

    def loop_body(kb, carry):
        step(kb, False)
        return carry

    lax.fori_loop(0, i, loop_body, 0)
    step(i, True)

    lp = lam_ref[...]
    lam = (jnp.exp(jnp.sum(lp[0:1] * lp[1:2], axis=1, keepdims=True))
           - jnp.exp(jnp.sum(lp[2:3] * lp[3:4], axis=1, keepdims=True)) + lambda_init)
    o = acc_ref[0] / l_ref[0] - lam * (acc_ref[1] / l_ref[1])
    o = o * lax.rsqrt(jnp.mean(jnp.square(o), axis=1, keepdims=True) + RMS_EPS)
    o_ref[...] = (o * g_ref[...] * (1.0 - lambda_init)).astype(BF16)


def _df_attention(q, k, v, lam_p, norm_g, lambda_init, B, S):
    T = q.shape[0]
    tq = _tile(S, 512)
    nq = S // tq
    k3 = k.reshape(B, S, WIDTH)
    v3 = v.reshape(B, S, WIDTH)
    kv_spec = pl.BlockSpec((None, S, HEAD_COLS), lambda b, h, i: (b, 0, h))
    return pl.pallas_call(
        functools.partial(_df_body, tq=tq, lambda_init=lambda_init),
        grid=(B, WIDTH // HEAD_COLS, nq),
        in_specs=[pl.BlockSpec(lam_p.shape, lambda b, h, i: (0, 0)),
                  pl.BlockSpec((1, HEAD_COLS), lambda b, h, i: (0, 0)),
                  pl.BlockSpec((tq, HEAD_COLS), lambda b, h, i: (b * nq + i, h)), kv_spec, kv_spec],
        out_specs=pl.BlockSpec((tq, HEAD_COLS), lambda b, h, i: (b * nq + i, h)),
        out_shape=jax.ShapeDtypeStruct((T, WIDTH), BF16),
        scratch_shapes=[pltpu.VMEM((2, tq, 1), F32), pltpu.VMEM((2, tq, 1), F32),
                        pltpu.VMEM((2, tq, HEAD_COLS), F32)],
        compiler_params=_cparams("parallel", "parallel", "arbitrary"),
        name="df_attention",
    )(lam_p, norm_g, q, k3, v3)


def _mem_kv_body(mem_ref, w_ref, k_ref, v_ref):
    kv = _dot(mem_ref[...].astype(BF16), w_ref[...])
    k_ref[...] = kv[:, :WIDTH].astype(BF16)
    v_ref[...] = kv[:, WIDTH:].astype(BF16)


def _mem_kv(mem2, w_kv):
    n, D = mem2.shape
    out = jax.ShapeDtypeStruct((n, WIDTH), BF16)
    return pl.pallas_call(
        _mem_kv_body,
        grid=(1,),
        in_specs=[pl.BlockSpec((n, D), lambda i: (0, 0)), pl.BlockSpec(w_kv.shape, lambda i: (0, 0))],
        out_specs=[pl.BlockSpec((n, WIDTH), lambda i: (0, 0))] * 2,
        out_shape=[out, out],
        compiler_params=_cparams("arbitrary"),
        name="mem_kv",
    )(mem2, w_kv)


def _mx_body(q_ref, k_ref, v_ref, o_ref):
    scale = MX_HEAD_DIM ** -0.5
    for h in range(WIDTH // MX_HEAD_DIM):
        cols = slice(h * MX_HEAD_DIM, (h + 1) * MX_HEAD_DIM)
        s = _dot_nt(q_ref[:, cols], k_ref[:, cols]) * scale
        p = jnp.exp(s - jnp.max(s, axis=1, keepdims=True))
        p = p / jnp.sum(p, axis=1, keepdims=True)
        o_ref[:, cols] = _dot(p.astype(BF16), v_ref[:, cols]).astype(BF16)


def _mx_attention(q, mem_k, mem_v, B, S, M):
    T = q.shape[0]
    tq = _tile(S, 512)
    nq = S // tq
    kv_spec = pl.BlockSpec((None, M, WIDTH), lambda i: (i // nq, 0, 0))
    return pl.pallas_call(
        _mx_body,
        grid=(T // tq,),
        in_specs=[pl.BlockSpec((tq, WIDTH), lambda i: (i, 0)), kv_spec, kv_spec],
        out_specs=pl.BlockSpec((tq, WIDTH), lambda i: (i, 0)),
        out_shape=jax.ShapeDtypeStruct((T, WIDTH), BF16),
        compiler_params=_cparams("parallel"),
        name="mx_attention",
    )(q, mem_k.reshape(B, M, WIDTH), mem_v.reshape(B, M, WIDTH))


def _layer_norm(y, g, b):
    mu = jnp.mean(y, axis=1, keepdims=True)
    yc = y - mu
    var = jnp.mean(jnp.square(yc), axis=1, keepdims=True)
    return yc * lax.rsqrt(var + LN_EPS) * g + b


def _merge_body(x_ref, osb_ref, odf_ref, omx_ref, wg_ref, bg_ref, wsb_ref, wdf_ref, wmx_ref, wo_ref,
                g_ref, b_ref, x1_ref, x1t_ref, *, alpha):
    x = x_ref[...]
    xb = x.astype(BF16)
    D = x.shape[1]
    merged = None
    for n, (o_ref, w_ref) in enumerate(((osb_ref, wsb_ref), (odf_ref, wdf_ref), (omx_ref, wmx_ref))):
        cols = slice(n * D, (n + 1) * D)
        gate = 1.0 / (1.0 + jnp.exp(-(_dot(xb, wg_ref[:, cols]) + bg_ref[:, cols])))
        term = gate * _dot(o_ref[...], w_ref[...])
        merged = term if merged is None else merged + term
    mix = _dot(merged.astype(BF16), wo_ref[...])
    x1 = _layer_norm(alpha * x + mix, g_ref[...], b_ref[...])
    x1_ref[...] = x1
    x1t_ref[...] = x1.T.astype(BF16)


def _merge(x2, o_sb, o_df, o_mx, w_gate, b_gate, w_sb, w_df, w_mx, w_out, g, b, alpha):
    T, D = x2.shape
    tm = _tile(T, 256)
    full = lambda a: pl.BlockSpec(a.shape, lambda i: (0, 0))
    tok = lambda w: pl.BlockSpec((tm, w), lambda i: (i, 0))
    return pl.pallas_call(
        functools.partial(_merge_body, alpha=alpha),
        grid=(T // tm,),
        in_specs=[tok(D), tok(WIDTH), tok(WIDTH), tok(WIDTH), full(w_gate), full(b_gate), full(w_sb),
                  full(w_df), full(w_mx), full(w_out), full(g), full(b)],
        out_specs=[tok(D), pl.BlockSpec((D, tm), lambda i: (0, i))],
        out_shape=[jax.ShapeDtypeStruct((T, D), F32), jax.ShapeDtypeStruct((D, T), BF16)],
        compiler_params=_cparams("parallel"),
        name="merge_ln",
    )(x2, o_sb, o_df, o_mx, w_gate, b_gate, w_sb, w_df, w_mx, w_out, g, b)


def _top16(s):
    idx = lax.broadcasted_iota(jnp.int32, s.shape, 0).astype(F32)
    work = s
    rank = jnp.full(s.shape, float(PEER_TOPK), F32)
    vals = []
    for r in range(PEER_TOPK):
        m = jnp.max(work, axis=0, keepdims=True)
        first = jnp.min(jnp.where(work == m, idx, float(s.shape[0])), axis=0, keepdims=True)
        hit = idx == first
        rank = jnp.where(hit, float(r), rank)
        work = jnp.where(hit, -jnp.inf, work)
        vals.append(m)
    return rank, vals


def _pair_select(v0, v1):
    K = PEER_TOPK
    t1 = jnp.concatenate(v1, axis=0)
    t0_tail = jnp.concatenate(v0[K // 2:], axis=0)
    sub = lax.broadcasted_iota(jnp.int32, (K // 2,) + t1.shape[1:], 0)
    subf = sub.astype(F32)
    groups = [(v0[0] + t1, lax.broadcasted_iota(jnp.int32, t1.shape, 0).astype(F32))]
    for a in range(1, K // 2):
        vals = jnp.where(sub < K // (a + 1), v0[a] + t1[:K // 2], -jnp.inf)
        groups.append((vals, subf + float(a * K)))
    groups.append((t0_tail + v1[0], (subf + float(K // 2)) * float(K)))
    top = v0[0] + v1[0]
    work = [g[0] for g in groups]
    sel = [jnp.zeros(g[0].shape, F32) for g in groups]
    for _ in range(K):
        m = functools.reduce(jnp.maximum, [jnp.max(w, axis=0, keepdims=True) for w in work])
        first = functools.reduce(jnp.minimum, [
            jnp.min(jnp.where(w == m, g[1], float(K * K)), axis=0, keepdims=True)
            for w, g in zip(work, groups)])
        for n, g in enumerate(groups):
            hit = g[1] == first
            sel[n] = jnp.where(hit, 1.0, sel[n])
            work[n] = jnp.where(hit, -jnp.inf, work[n])
    nb = jnp.concatenate([jnp.sum(s, axis=0, keepdims=True) for s in sel[:-1]] + [sel[-1]], axis=0)
    z = functools.reduce(jnp.add, [
        jnp.sum(jnp.where(s > 0.0, jnp.exp(g[0] - top), 0.0), axis=0, keepdims=True)
        for s, g in zip(sel, groups)])
    return nb, z


def _route_body(xt_ref, wq_ref, keys_ref, a_ref, nbi_ref, bw_ref, r1_ref, q_scr):
    q_scr[...] = _dot(wq_ref[...], xt_ref[...]).astype(BF16)

    def head(h, carry):
        scores = []
        for p in range(2):
            hp = h * 2 + p
            q_hp = q_scr[pl.ds(pl.multiple_of(hp * PEER_HALF, PEER_HALF), PEER_HALF), :]
            scores.append(_dot(keys_ref[hp], q_hp))
        rank0, v0 = _top16(scores[0])
        rank1, v1 = _top16(scores[1])
        nb, z = _pair_select(v0, v1)
        nbi = jnp.zeros_like(rank0)
        for a in range(PEER_TOPK):
            nbi = jnp.where(rank0 == float(a), nb[a:a + 1, :], nbi)
        keep0 = rank0 < float(PEER_TOPK)
        keep1 = rank1 < float(PEER_TOPK)
        a_ref[h] = jnp.where(keep0, jnp.exp(scores[0] - v0[0]), 0.0)
        nbi_ref[h] = nbi
        bw_ref[h] = jnp.where(keep1, jnp.exp(scores[1] - v1[0]) / z, 0.0)
        r1_ref[h] = rank1
        return carry

    lax.fori_loop(0, PEER_HEADS, head, 0)


def _route(x1t, wq_t, keys):
    D, T = x1t.shape
    tm = _tile(T, 256)
    out = jax.ShapeDtypeStruct((PEER_HEADS, PEER_NKEYS, T), F32)
    blk = pl.BlockSpec((PEER_HEADS, PEER_NKEYS, tm), lambda i: (0, 0, i))
    return pl.pallas_call(
        _route_body,
        grid=(T // tm,),
        in_specs=[pl.BlockSpec((D, tm), lambda i: (0, i)),
                  pl.BlockSpec(wq_t.shape, lambda i: (0, 0)),
                  pl.BlockSpec(keys.shape, lambda i: (0, 0, 0))],
        out_specs=[blk] * 4,
        out_shape=[out] * 4,
        scratch_shapes=[pltpu.VMEM((wq_t.shape[0], tm), BF16)],
        compiler_params=_cparams("parallel"),
        name="peer_route",
    )(x1t, wq_t, keys)


def _tables_body(u_ref, v_ref, ub_ref, vt_ref):
    ub_ref[...] = u_ref[...].astype(BF16)
    vt_ref[...] = v_ref[...].T.astype(BF16)


def _tables(u_tab, v_tab):
    E, D = u_tab.shape
    te = _tile(E, 512)
    return pl.pallas_call(
        _tables_body,
        grid=(E // te,),
        in_specs=[pl.BlockSpec((te, D), lambda e: (e, 0))] * 2,
        out_specs=[pl.BlockSpec((te, D), lambda e: (e, 0)), pl.BlockSpec((D, te), lambda e: (0, e))],
        out_shape=[jax.ShapeDtypeStruct((E, D), BF16), jax.ShapeDtypeStruct((D, E), BF16)],
        compiler_params=_cparams("parallel"),
        name="peer_tables",
    )(u_tab, v_tab)


def _experts_body(xt_ref, u_ref, vt_ref, a_ref, nbi_ref, bw_ref, r1_ref, x1_ref, g_ref, b_ref, o_ref,
                  acc_ref, h_ref, act_ref, *, alpha, rows_per_step):
    e = pl.program_id(1)

    @pl.when(e == 0)
    def _():
        acc_ref[...] = jnp.zeros_like(acc_ref)

    h_ref[...] = _dot(u_ref[...], xt_ref[...])

    def row(r, carry):
        i = e * rows_per_step + r
        off = pl.multiple_of(r * PEER_NKEYS, PEER_NKEYS)
        hid = h_ref[pl.ds(off, PEER_NKEYS), :]
        gate = jnp.zeros_like(hid)
        for hd in range(PEER_HEADS):
            a = a_ref[hd, pl.ds(i, 1), :]
            nbi = nbi_ref[hd, pl.ds(i, 1), :]
            gate = gate + a * jnp.where(r1_ref[hd] < nbi, bw_ref[hd], 0.0)
        gelu = 0.5 * hid * (1.0 + lax.erf(hid * (2.0 ** -0.5)))
        act_ref[pl.ds(off, PEER_NKEYS), :] = (gelu * gate).astype(BF16)
        return carry

    lax.fori_loop(0, rows_per_step, row, 0)
    acc_ref[...] += _dot(vt_ref[...], act_ref[...])

    @pl.when(e == pl.num_programs(1) - 1)
    def _():
        y = alpha * x1_ref[...] + acc_ref[...].T
        o_ref[...] = _layer_norm(y, g_ref[...], b_ref[...])


def _experts(x1, x1t, u_b, v_t, a, nbi, bw, r1, g, b, alpha):
    T, D = x1.shape
    E = u_b.shape[0]
    tm = _tile(T, 512)
    te = _tile(E, 1024)
    route_spec = pl.BlockSpec((PEER_HEADS, PEER_NKEYS, tm), lambda t, e: (0, 0, t))
    vec = lambda arr: pl.BlockSpec(arr.shape, lambda t, e: (0, 0))
    return pl.pallas_call(
        functools.partial(_experts_body, alpha=alpha, rows_per_step=te // PEER_NKEYS),
        grid=(T // tm, E // te),
        in_specs=[pl.BlockSpec((D, tm), lambda t, e: (0, t)),
                  pl.BlockSpec((te, D), lambda t, e: (e, 0)),
                  pl.BlockSpec((D, te), lambda t, e: (0, e)),
                  route_spec, route_spec, route_spec, route_spec,
                  pl.BlockSpec((tm, D), lambda t, e: (t, 0)), vec(g), vec(b)],
        out_specs=pl.BlockSpec((tm, D), lambda t, e: (t, 0)),
        out_shape=jax.ShapeDtypeStruct((T, D), F32),
        scratch_shapes=[pltpu.VMEM((D, tm), F32), pltpu.VMEM((te, tm), F32), pltpu.VMEM((te, tm), BF16)],
        compiler_params=_cparams("parallel", "arbitrary"),
        name="peer_experts",
    )(x1t, u_b, v_t, a, nbi, bw, r1, x1, g, b)


def kernel(x, mem, positions, w_in, w_mem_kv, df_lambda, df_norm_g, w_gate, b_gate, w_br_sb, w_br_df,
           w_br_mx, w_out, ln1_g, ln1_b, w_query, sub_keys, u_tab, v_tab, ln2_g, ln2_b):
    B, S, D = x.shape
    M = mem.shape[1]
    depth = w_in.shape[0]
    T = B * S
    alpha = (2.0 * depth) ** 0.25

    inv = ROPE_THETA ** (-jnp.arange(0, ROPE_DIM, 2, dtype=F32) / ROPE_DIM)
    d = jnp.arange(LANES) % HALF_COLS
    inv_lane = jnp.where(d < ROPE_DIM, inv[d % (ROPE_DIM // 2)], 0.0).reshape(1, LANES).astype(F32)
    pos2 = positions.reshape(T, 1)
    mem2 = mem.reshape(B * M, D)
    x2 = x.reshape(T, D)
    row = lambda v: v.reshape(1, -1)

    for l in range(depth):
        lambda_init = 0.8 - 0.6 * math.exp(-0.3 * l)
        sbq, sbk, sbv, dfq, dfk, dfv, mxq = _in_proj(x2, pos2, inv_lane, w_in[l].astype(BF16))
        o_sb = _sb_attention(sbq, sbk, sbv, B, S)
        o_df = _df_attention(dfq, dfk, dfv, df_lambda[l], row(df_norm_g[l]), lambda_init, B, S)
        mem_k, mem_v = _mem_kv(mem2, w_mem_kv[l].astype(BF16))
        o_mx = _mx_attention(mxq, mem_k, mem_v, B, S, M)
        x1, x1t = _merge(x2, o_sb, o_df, o_mx, w_gate[l].astype(BF16), row(b_gate[l]),
                         w_br_sb[l].astype(BF16), w_br_df[l].astype(BF16), w_br_mx[l].astype(BF16),
                         w_out[l].astype(BF16), row(ln1_g[l]), row(ln1_b[l]), alpha)
        keys = sub_keys[l].reshape(PEER_HEADS * 2, PEER_NKEYS, PEER_HALF).astype(BF16)
        a, nbi, bw, r1 = _route(x1t, w_query[l].T.astype(BF16), keys)
        u_b, v_t = _tables(u_tab[l], v_tab[l])
        x2 = _experts(x1, x1t, u_b, v_t, a, nbi, bw, r1, row(ln2_g[l]), row(ln2_b[l]), alpha)
    return x2.reshape(B, S, D)
```

```python
import functools
import math

import jax
import jax.numpy as jnp
from jax import lax
from jax.experimental import pallas as pl
from jax.experimental.pallas import tpu as pltpu

F32 = jnp.float32
BF16 = jnp.bfloat16

WIDTH = 512
N_GROUPS = 7
HEAD_COLS = 128
HALF_COLS = 64
ROPE_THETA = 500000.0
ROPE_DIM = 16
MX_HEAD_DIM = 128
DF_ONES_ROWS = 16
PEER_HEADS = 8
PEER_NKEYS = 128
PEER_TOPK = 16
PEER_HALF = 128
LN_EPS = 1e-5
RMS_EPS = 1e-5
SB_SKIP_BELOW = -110.0

VMEM_LIMIT_BYTES = 56 * 1024 * 1024
LANES = 128
MXU_DEPTH = 256
BF16_SUBLANES = 16


def _cparams(*sem):
    return pltpu.CompilerParams(dimension_semantics=sem, vmem_limit_bytes=VMEM_LIMIT_BYTES)


def _tile(n, want):
    t = min(n, want)
    while n % t:
        t //= 2
    return t


def _dot(a, b):
    return jnp.dot(a, b, preferred_element_type=F32)


def _dot_nt(a, b):
    return lax.dot_general(a, b, (((1,), (1,)), ((), ())), preferred_element_type=F32)


def _in_proj_body(x_ref, pos_ref, inv_ref, w_ref, sbq, sbk, sbv, dfq, dfk, dfv, mxq):
    xb = x_ref[...].astype(BF16)
    ang = pos_ref[...].astype(F32) * inv_ref[...]
    d = lax.broadcasted_iota(jnp.int32, ang.shape, 1) & (HALF_COLS - 1)
    first = d < ROPE_DIM // 2
    rot = d < ROPE_DIM
    cos = jnp.where(rot, jnp.cos(ang), 1.0)
    sin = jnp.sin(ang)
    sin = jnp.where(first, -sin, jnp.where(rot, sin, 0.0))

    def proj(g):
        return _dot(xb, w_ref[:, g * WIDTH:(g + 1) * WIDTH])

    def rotary(p):
        outs = []
        for j in range(WIDTH // LANES):
            xj = p[:, j * LANES:(j + 1) * LANES]
            partner = jnp.where(first, pltpu.roll(xj, LANES - ROPE_DIM // 2, 1),
                                pltpu.roll(xj, ROPE_DIM // 2, 1))
            outs.append(xj * cos + partner * sin)
        return jnp.concatenate(outs, axis=1)

    scale = HALF_COLS ** -0.5
    sbq[...] = (proj(0) * scale).astype(BF16)
    sbk[...] = proj(1).astype(BF16)
    sbv[...] = proj(2).astype(BF16)
    dfq[...] = (rotary(proj(3)) * scale).astype(BF16)
    dfk[...] = rotary(proj(4)).astype(BF16)
    vt = proj(5).T
    for h in range(WIDTH // HEAD_COLS):
        dfv[0, h, :HEAD_COLS, :] = vt[h * HEAD_COLS:(h + 1) * HEAD_COLS].astype(BF16)
        dfv[0, h, HEAD_COLS:, :] = jnp.ones((DF_ONES_ROWS, vt.shape[1]), BF16)
    mxq[...] = proj(6).astype(BF16)


def _in_proj(x2, pos2, inv_lane, w_in, tm):
    T, D = x2.shape
    out = jax.ShapeDtypeStruct((T, WIDTH), BF16)
    blk = pl.BlockSpec((tm, WIDTH), lambda i: (i, 0))
    heads = WIDTH // HEAD_COLS
    vt_rows = HEAD_COLS + DF_ONES_ROWS
    vt_out = jax.ShapeDtypeStruct((T // tm, heads, vt_rows, tm), BF16)
    vt_blk = pl.BlockSpec((1, heads, vt_rows, tm), lambda i: (i, 0, 0, 0))
    return pl.pallas_call(
        _in_proj_body,
        grid=(T // tm,),
        in_specs=[pl.BlockSpec((tm, D), lambda i: (i, 0)),
                  pl.BlockSpec((tm, 1), lambda i: (i, 0)),
                  pl.BlockSpec((1, LANES), lambda i: (0, 0)),
                  pl.BlockSpec((D, N_GROUPS * WIDTH), lambda i: (0, 0))],
        out_specs=[blk] * 5 + [vt_blk, blk],
        out_shape=[out] * 5 + [vt_out, out],
        compiler_params=_cparams("parallel"),
        name="in_proj",
    )(x2, pos2, inv_lane, w_in)


def _sb_body(q_ref, k_ref, v_ref, o_ref, acc_ref, c_ref, *, tq):
    i = pl.program_id(2)
    qf = q_ref[...].astype(F32)
    lane = lax.broadcasted_iota(jnp.int32, qf.shape, 1)
    low = lane < HALF_COLS
    qs = (jnp.where(low, qf, 0.0).astype(BF16), jnp.where(low, 0.0, qf).astype(BF16))
    row = lax.broadcasted_iota(jnp.int32, (tq, tq), 0)
    col = lax.broadcasted_iota(jnp.int32, (tq, tq), 1)
    later = (row > col).astype(BF16)
    acc_ref[...] = jnp.zeros_like(acc_ref)
    c_ref[...] = jnp.zeros_like(c_ref)

    def cond(carry):
        kb, cmax = carry
        return jnp.logical_and(kb >= 0, cmax > SB_SKIP_BELOW)

    def body(carry):
        kb, _ = carry
        start = pl.multiple_of(kb * tq, tq)
        k = k_ref[pl.ds(start, tq), :]
        v = v_ref[pl.ds(start, tq), :]
        causal = (kb * tq + col) < (i * tq + row)
        cmax = None
        for h in range(2):
            z = _dot_nt(qs[h], k)
            t = jnp.log1p(jnp.exp(-jnp.abs(z)))
            log_beta = jnp.minimum(z, 0.0) - t
            lom = jnp.where(causal, -jnp.maximum(z, 0.0) - t, 0.0)
            hi = lom.astype(BF16)
            lo = (lom - hi.astype(F32)).astype(BF16)
            rest = _dot(hi, later) + _dot(lo, later)
            c = c_ref[h]
            a = jnp.where(causal, jnp.exp(log_beta + rest + c), 0.0)
            acc_ref[h] += _dot(a.astype(BF16), v)
            c_new = c + rest[:, 0:1] + lom[:, 0:1]
            c_ref[h] = c_new
            m = jnp.max(c_new)
            cmax = m if cmax is None else jnp.maximum(cmax, m)
        return kb - 1, cmax

    lax.while_loop(cond, body, (i, jnp.float32(0.0)))
    o_ref[...] = jnp.where(low, acc_ref[0], acc_ref[1]).astype(BF16)


def _sb_attention(q, k, v, B, S):
    T = q.shape[0]
    tq = _tile(S, 256)
    nq = S // tq
    k3 = k.reshape(B, S, WIDTH)
    v3 = v.reshape(B, S, WIDTH)
    kv_spec = pl.BlockSpec((None, S, HEAD_COLS), lambda b, p, i: (b, 0, p))
    return pl.pallas_call(
        functools.partial(_sb_body, tq=tq),
        grid=(B, WIDTH // HEAD_COLS, nq),
        in_specs=[pl.BlockSpec((tq, HEAD_COLS), lambda b, p, i: (b * nq + i, p)), kv_spec, kv_spec],
        out_specs=pl.BlockSpec((tq, HEAD_COLS), lambda b, p, i: (b * nq + i, p)),
        out_shape=jax.ShapeDtypeStruct((T, WIDTH), BF16),
        scratch_shapes=[pltpu.VMEM((2, tq, HEAD_COLS), F32), pltpu.VMEM((2, tq, 1), F32)],
        compiler_params=_cparams("parallel", "parallel", "arbitrary"),
        name="sb_attention",
    )(q, k3, v3)


def _df_body(lam_ref, g_ref, q_ref, k_ref, vt_ref, o_ref, m_ref, acc_ref, *, tq, lambda_init):
    i = pl.program_id(2)
    qt = q_ref[...].astype(F32).T
    sub = lax.broadcasted_iota(jnp.int32, qt.shape, 0)
    low = sub < HALF_COLS
    qts = (jnp.where(low, qt, 0.0).astype(BF16), jnp.where(low, 0.0, qt).astype(BF16))
    m_ref[...] = jnp.full_like(m_ref, -jnp.inf)
    acc_ref[...] = jnp.zeros_like(acc_ref)

    def step(kb, diagonal):
        start = pl.multiple_of(kb * tq, tq)
        k = k_ref[pl.ds(start, tq), :]
        vt = vt_ref[kb]
        for m in range(2):
            s = _dot(k, qts[m])
            if diagonal:
                key = lax.broadcasted_iota(jnp.int32, (tq, tq), 0)
                qry = lax.broadcasted_iota(jnp.int32, (tq, tq), 1)
                s = jnp.where(key <= qry, s, -jnp.inf)
            m_old = m_ref[m]
            m_new = jnp.maximum(m_old, jnp.max(s, axis=0, keepdims=True))
            p = jnp.exp(s - m_new).astype(BF16)
            acc_ref[m] = jnp.exp(m_old - m_new) * acc_ref[m] + _dot(vt, p)
            m_ref[m] = m_new

    def loop_body(kb, carry):
        step(kb, False)
        return carry

    lax.fori_loop(0, i, loop_body, 0)
    step(i, True)

    lp = lam_ref[...]
    lam = (jnp.exp(jnp.sum(lp[0:1] * lp[1:2], axis=1, keepdims=True))
           - jnp.exp(jnp.sum(lp[2:3] * lp[3:4], axis=1, keepdims=True)) + lambda_init)
    outs = [acc_ref[m, :HEAD_COLS, :] / acc_ref[m, HEAD_COLS:HEAD_COLS + 1, :] for m in range(2)]
    o = (outs[0] - lam * outs[1]).T
    o = o * lax.rsqrt(jnp.mean(jnp.square(o), axis=1, keepdims=True) + RMS_EPS)
    o_ref[...] = (o * g_ref[...] * (1.0 - lambda_init)).astype(BF16)


def _df_attention(q, k, vt, lam_p, norm_g, lambda_init, B, S, tq):
    T = q.shape[0]
    nq = S // tq
    k3 = k.reshape(B, S, WIDTH)
    vt_rows = vt.shape[2]
    return pl.pallas_call(
        functools.partial(_df_body, tq=tq, lambda_init=lambda_init),
        grid=(B, WIDTH // HEAD_COLS, nq),
        in_specs=[pl.BlockSpec(lam_p.shape, lambda b, h, i: (0, 0)),
                  pl.BlockSpec((1, HEAD_COLS), lambda b, h, i: (0, 0)),
                  pl.BlockSpec((tq, HEAD_COLS), lambda b, h, i: (b * nq + i, h)),
                  pl.BlockSpec((None, S, HEAD_COLS), lambda b, h, i: (b, 0, h)),
                  pl.BlockSpec((nq, None, vt_rows, tq), lambda b, h, i: (b, h, 0, 0))],
        out_specs=pl.BlockSpec((tq, HEAD_COLS), lambda b, h, i: (b * nq + i, h)),
        out_shape=jax.ShapeDtypeStruct((T, WIDTH), BF16),
        scratch_shapes=[pltpu.VMEM((2, 1, tq), F32), pltpu.VMEM((2, vt_rows, tq), F32)],
        compiler_params=_cparams("parallel", "parallel", "arbitrary"),
        name="df_attention",
    )(lam_p, norm_g, q, k3, vt)


def _mem_kv_body(mem_ref, w_ref, k_ref, v_ref):
    kv = _dot(mem_ref[...].astype(BF16), w_ref[...])
    k_ref[...] = kv[:, :WIDTH].astype(BF16)
    v_ref[...] = kv[:, WIDTH:].astype(BF16)


def _mem_kv(mem2, w_kv):
    n, D = mem2.shape
    out = jax.ShapeDtypeStruct((n, WIDTH), BF16)
    return pl.pallas_call(
        _mem_kv_body,
        grid=(1,),
        in_specs=[pl.BlockSpec((n, D), lambda i: (0, 0)), pl.BlockSpec(w_kv.shape, lambda i: (0, 0))],
        out_specs=[pl.BlockSpec((n, WIDTH), lambda i: (0, 0))] * 2,
        out_shape=[out, out],
        compiler_params=_cparams("arbitrary"),
        name="mem_kv",
    )(mem2, w_kv)


def _mx_body(q_ref, k_ref, v_ref, o_ref):
    scale = MX_HEAD_DIM ** -0.5
    for h in range(WIDTH // MX_HEAD_DIM):
        cols = slice(h * MX_HEAD_DIM, (h + 1) * MX_HEAD_DIM)
        s = _dot_nt(q_ref[:, cols], k_ref[:, cols]) * scale
        p = jnp.exp(s - jnp.max(s, axis=1, keepdims=True))
        p = p / jnp.sum(p, axis=1, keepdims=True)
        o_ref[:, cols] = _dot(p.astype(BF16), v_ref[:, cols]).astype(BF16)


def _mx_attention(q, mem_k, mem_v, B, S, M):
    T = q.shape[0]
    tq = _tile(S, 512)
    nq = S // tq
    kv_spec = pl.BlockSpec((None, M, WIDTH), lambda i: (i // nq, 0, 0))
    return pl.pallas_call(
        _mx_body,
        grid=(T // tq,),
        in_specs=[pl.BlockSpec((tq, WIDTH), lambda i: (i, 0)), kv_spec, kv_spec],
        out_specs=pl.BlockSpec((tq, WIDTH), lambda i: (i, 0)),
        out_shape=jax.ShapeDtypeStruct((T, WIDTH), BF16),
        compiler_params=_cparams("parallel"),
        name="mx_attention",
    )(q, mem_k.reshape(B, M, WIDTH), mem_v.reshape(B, M, WIDTH))


def _layer_norm(y, g, b):
    mu = jnp.mean(y, axis=1, keepdims=True)
    yc = y - mu
    var = jnp.mean(jnp.square(yc), axis=1, keepdims=True)
    return yc * lax.rsqrt(var + LN_EPS) * g + b


def _merge_body(x_ref, osb_ref, odf_ref, omx_ref, wg_ref, bg_ref, wsb_ref, wdf_ref, wmx_ref, wo_ref,
                g_ref, b_ref, x1_ref, x1t_ref, *, alpha):
    x = x_ref[...]
    xb = x.astype(BF16)
    D = x.shape[1]
    merged = None
    for n, (o_ref, w_ref) in enumerate(((osb_ref, wsb_ref), (odf_ref, wdf_ref), (omx_ref, wmx_ref))):
        cols = slice(n * D, (n + 1) * D)
        gate = 1.0 / (1.0 + jnp.exp(-(_dot(xb, wg_ref[:, cols]) + bg_ref[:, cols])))
        term = gate * _dot(o_ref[...], w_ref[...])
        merged = term if merged is None else merged + term
    mix = _dot(merged.astype(BF16), wo_ref[...])
    x1 = _layer_norm(alpha * x + mix, g_ref[...], b_ref[...])
    x1_ref[...] = x1
    x1t_ref[...] = x1.T.astype(BF16)


def _merge(x2, o_sb, o_df, o_mx, w_gate, b_gate, w_sb, w_df, w_mx, w_out, g, b, alpha):
    T, D = x2.shape
    tm = _tile(T, 256)
    full = lambda a: pl.BlockSpec(a.shape, lambda i: (0, 0))
    tok = lambda w: pl.BlockSpec((tm, w), lambda i: (i, 0))
    return pl.pallas_call(
        functools.partial(_merge_body, alpha=alpha),
        grid=(T // tm,),
        in_specs=[tok(D), tok(WIDTH), tok(WIDTH), tok(WIDTH), full(w_gate), full(b_gate), full(w_sb),
                  full(w_df), full(w_mx), full(w_out), full(g), full(b)],
        out_specs=[tok(D), pl.BlockSpec((D, tm), lambda i: (0, i))],
        out_shape=[jax.ShapeDtypeStruct((T, D), F32), jax.ShapeDtypeStruct((D, T), BF16)],
        compiler_params=_cparams("parallel"),
        name="merge_ln",
    )(x2, o_sb, o_df, o_mx, w_gate, b_gate, w_sb, w_df, w_mx, w_out, g, b)


def _top16(s):
    idx = lax.broadcasted_iota(jnp.int32, s.shape, 0).astype(F32)
    work = s
    rank = jnp.full(s.shape, float(PEER_TOPK), F32)
    vals = []
    for r in range(PEER_TOPK):
        m = jnp.max(work, axis=0, keepdims=True)
        first = jnp.min(jnp.where(work == m, idx, float(s.shape[0])), axis=0, keepdims=True)
        hit = idx == first
        rank = jnp.where(hit, float(r), rank)
        work = jnp.where(hit, -jnp.inf, work)
        vals.append(m)
    return rank, vals


def _pair_select(v0, v1):
    K = PEER_TOPK
    t1 = jnp.concatenate(v1, axis=0)
    t0_tail = jnp.concatenate(v0[K // 2:], axis=0)
    sub = lax.broadcasted_iota(jnp.int32, (K // 2,) + t1.shape[1:], 0)
    subf = sub.astype(F32)
    groups = [(v0[0] + t1, lax.broadcasted_iota(jnp.int32, t1.shape, 0).astype(F32))]
    for a in range(1, K // 2):
        vals = jnp.where(sub < K // (a + 1), v0[a] + t1[:K // 2], -jnp.inf)
        groups.append((vals, subf + float(a * K)))
    groups.append((t0_tail + v1[0], (subf + float(K // 2)) * float(K)))
    top = v0[0] + v1[0]
    work = [g[0] for g in groups]
    sel = [jnp.zeros(g[0].shape, F32) for g in groups]
    for _ in range(K):
        m = functools.reduce(jnp.maximum, [jnp.max(w, axis=0, keepdims=True) for w in work])
        first = functools.reduce(jnp.minimum, [
            jnp.min(jnp.where(w == m, g[1], float(K * K)), axis=0, keepdims=True)
            for w, g in zip(work, groups)])
        for n, g in enumerate(groups):
            hit = g[1] == first
            sel[n] = jnp.where(hit, 1.0, sel[n])
            work[n] = jnp.where(hit, -jnp.inf, work[n])
    nb = jnp.concatenate([jnp.sum(s, axis=0, keepdims=True) for s in sel[:-1]] + [sel[-1]], axis=0)
    z = functools.reduce(jnp.add, [
        jnp.sum(jnp.where(s > 0.0, jnp.exp(g[0] - top), 0.0), axis=0, keepdims=True)
        for s, g in zip(sel, groups)])
    return nb, z


def _route_body(xt_ref, wq_ref, keys_ref, a_ref, nbi_ref, bw_ref, r1_ref, q_scr):
    q_scr[...] = _dot(wq_ref[...], xt_ref[...]).astype(BF16)

    def head(h, carry):
        scores = []
        for p in range(2):
            hp = h * 2 + p
            q_hp = q_scr[pl.ds(pl.multiple_of(hp * PEER_HALF, PEER_HALF), PEER_HALF), :]
            scores.append(_dot(keys_ref[hp], q_hp))
        rank0, v0 = _top16(scores[0])
        rank1, v1 = _top16(scores[1])
        nb, z = _pair_select(v0, v1)
        nbi = jnp.zeros_like(rank0)
        for a in range(PEER_TOPK):
            nbi = jnp.where(rank0 == float(a), nb[a:a + 1, :], nbi)
        keep0 = rank0 < float(PEER_TOPK)
        keep1 = rank1 < float(PEER_TOPK)
        a_ref[h] = jnp.where(keep0, jnp.exp(scores[0] - v0[0]), 0.0)
        nbi_ref[h] = nbi
        bw_ref[h] = jnp.where(keep1, jnp.exp(scores[1] - v1[0]) / z, 0.0).astype(BF16)
        r1_ref[h] = rank1.astype(BF16)
        return carry

    lax.fori_loop(0, PEER_HEADS, head, 0)


def _route(x1t, wq_t, keys):
    D, T = x1t.shape
    tm = _tile(T, 256)
    out = jax.ShapeDtypeStruct((PEER_HEADS, PEER_NKEYS, T), F32)
    out_b = jax.ShapeDtypeStruct((PEER_HEADS, PEER_NKEYS, T), BF16)
    blk = pl.BlockSpec((PEER_HEADS, PEER_NKEYS, tm), lambda i: (0, 0, i))
    return pl.pallas_call(
        _route_body,
        grid=(T // tm,),
        in_specs=[pl.BlockSpec((D, tm), lambda i: (0, i)),
                  pl.BlockSpec(wq_t.shape, lambda i: (0, 0)),
                  pl.BlockSpec(keys.shape, lambda i: (0, 0, 0))],
        out_specs=[blk] * 4,
        out_shape=[out, out, out_b, out_b],
        scratch_shapes=[pltpu.VMEM((wq_t.shape[0], tm), BF16)],
        compiler_params=_cparams("parallel"),
        name="peer_route",
    )(x1t, wq_t, keys)


def _tables_body(u_ref, v_ref, ub_ref, vt_ref):
    ub_ref[...] = u_ref[...].astype(BF16)
    vt_ref[...] = v_ref[...].T.astype(BF16)


def _tables(u_tab, v_tab):
    E, D = u_tab.shape
    te = _tile(E, 512)
    return pl.pallas_call(
        _tables_body,
        grid=(E // te,),
        in_specs=[pl.BlockSpec((te, D), lambda e: (e, 0))] * 2,
        out_specs=[pl.BlockSpec((te, D), lambda e: (e, 0)), pl.BlockSpec((D, te), lambda e: (0, e))],
        out_shape=[jax.ShapeDtypeStruct((E, D), BF16), jax.ShapeDtypeStruct((D, E), BF16)],
        compiler_params=_cparams("parallel"),
        name="peer_tables",
    )(u_tab, v_tab)


def _token_row_bf16(ref, hd, i):
    tile = jnp.broadcast_to(ref[hd, pl.ds(i, 1), :], (BF16_SUBLANES, ref.shape[2])).astype(BF16)
    return jnp.concatenate([tile] * (PEER_NKEYS // BF16_SUBLANES), axis=0)


def _experts_body(xt_ref, u_ref, vt_ref, a_ref, nbi_ref, bw_ref, r1_ref, x1_ref, g_ref, b_ref, o_ref,
                  acc_ref, *, alpha, rows_per_step):
    e = pl.program_id(1)

    @pl.when(e == 0)
    def _():
        acc_ref[...] = jnp.zeros_like(acc_ref)

    xt = xt_ref[...]
    acts = []
    for c in range(rows_per_step * PEER_NKEYS // MXU_DEPTH):
        hid_c = _dot(u_ref[c * MXU_DEPTH:(c + 1) * MXU_DEPTH, :], xt)
        for r in range(MXU_DEPTH // PEER_NKEYS):
            i = e * rows_per_step + c * (MXU_DEPTH // PEER_NKEYS) + r
            hid = hid_c[r * PEER_NKEYS:(r + 1) * PEER_NKEYS]
            gate = None
            for hd in range(PEER_HEADS):
                a = _token_row_bf16(a_ref, hd, i)
                nbi = _token_row_bf16(nbi_ref, hd, i)
                term = a * jnp.where(r1_ref[hd] < nbi, bw_ref[hd], jnp.zeros((), BF16))
                gate = term if gate is None else gate + term
            gelu = 0.5 * hid * (1.0 + lax.erf(hid * (2.0 ** -0.5)))
            acts.append(gelu.astype(BF16) * gate)
    acc_ref[...] += _dot(vt_ref[...], jnp.concatenate(acts, axis=0))

    @pl.when(e == pl.num_programs(1) - 1)
    def _():
        y = alpha * x1_ref[...] + acc_ref[...].T
        o_ref[...] = _layer_norm(y, g_ref[...], b_ref[...])


def _experts(x1, x1t, u_b, v_t, a, nbi, bw, r1, g, b, alpha):
    T, D = x1.shape
    E = u_b.shape[0]
    tm = _tile(T, 512)
    te = _tile(E, 1024)
    route_spec = pl.BlockSpec((PEER_HEADS, PEER_NKEYS, tm), lambda t, e: (0, 0, t))
    vec = lambda arr: pl.BlockSpec(arr.shape, lambda t, e: (0, 0))
    return pl.pallas_call(
        functools.partial(_experts_body, alpha=alpha, rows_per_step=te // PEER_NKEYS),
        grid=(T // tm, E // te),
        in_specs=[pl.BlockSpec((D, tm), lambda t, e: (0, t)),
                  pl.BlockSpec((te, D), lambda t, e: (e, 0)),
                  pl.BlockSpec((D, te), lambda t, e: (0, e)),
                  route_spec, route_spec, route_spec, route_spec,
                  pl.BlockSpec((tm, D), lambda t, e: (t, 0)), vec(g), vec(b)],
        out_specs=pl.BlockSpec((tm, D), lambda t, e: (t, 0)),
        out_shape=jax.ShapeDtypeStruct((T, D), F32),
        scratch_shapes=[pltpu.VMEM((D, tm), F32)],
        compiler_params=_cparams("parallel", "arbitrary"),
        name="peer_experts",
    )(x1t, u_b, v_t, a, nbi, bw, r1, x1, g, b)


def kernel(x, mem, positions, w_in, w_mem_kv, df_lambda, df_norm_g, w_gate, b_gate, w_br_sb, w_br_df,
           w_br_mx, w_out, ln1_g, ln1_b, w_query, sub_keys, u_tab, v_tab, ln2_g, ln2_b):
    B, S, D = x.shape
    M = mem.shape[1]
    depth = w_in.shape[0]
    T = B * S
    alpha = (2.0 * depth) ** 0.25

    inv = ROPE_THETA ** (-jnp.arange(0, ROPE_DIM, 2, dtype=F32) / ROPE_DIM)
    d = jnp.arange(LANES) % HALF_COLS
    inv_lane = jnp.where(d < ROPE_DIM, inv[d % (ROPE_DIM // 2)], 0.0).reshape(1, LANES).astype(F32)
    pos2 = positions.reshape(T, 1)
    mem2 = mem.reshape(B * M, D)
    x2 = x.reshape(T, D)
    row = lambda v: v.reshape(1, -1)

    for l in range(depth):
        lambda_init = 0.8 - 0.6 * math.exp(-0.3 * l)
        t_df = _tile(S, 512)
        sbq, sbk, sbv, dfq, dfk, dfvt, mxq = _in_proj(x2, pos2, inv_lane, w_in[l].astype(BF16), t_df)
        o_sb = _sb_attention(sbq, sbk, sbv, B, S)
        o_df = _df_attention(dfq, dfk, dfvt, df_lambda[l], row(df_norm_g[l]), lambda_init, B, S, t_df)
        mem_k, mem_v = _mem_kv(mem2, w_mem_kv[l].astype(BF16))
        o_mx = _mx_attention(mxq, mem_k, mem_v, B, S, M)
        x1, x1t = _merge(x2, o_sb, o_df, o_mx, w_gate[l].astype(BF16), row(b_gate[l]),
                         w_br_sb[l].astype(BF16), w_br_df[l].astype(BF16), w_br_mx[l].astype(BF16),
                         w_out[l].astype(BF16), row(ln1_g[l]), row(ln1_b[l]), alpha)
        keys = sub_keys[l].reshape(PEER_HEADS * 2, PEER_NKEYS, PEER_HALF).astype(BF16)
        a, nbi, bw, r1 = _route(x1t, w_query[l].T.astype(BF16), keys)
        u_b, v_t = _tables(u_tab[l], v_tab[l])
        x2 = _experts(x1, x1t, u_b, v_t, a, nbi, bw, r1, row(ln2_g[l]), row(ln2_b[l]), alpha)
    return x2.reshape(B, S, D)
```

```python
import functools
import math

import jax
import jax.numpy as jnp
from jax import lax
from jax.experimental import pallas as pl
from jax.experimental.pallas import tpu as pltpu

F32 = jnp.float32
BF16 = jnp.bfloat16

WIDTH = 512
N_GROUPS = 7
HEAD_COLS = 128
HALF_COLS = 64
ROPE_THETA = 500000.0
ROPE_DIM = 16
MX_HEAD_DIM = 128
DF_ONES_ROWS = 16
PEER_HEADS = 8
PEER_NKEYS = 128
PEER_TOPK = 16
PEER_HALF = 128
LN_EPS = 1e-5
RMS_EPS = 1e-5
SB_SKIP_BELOW = -110.0

VMEM_LIMIT_BYTES = 56 * 1024 * 1024
LANES = 128
MXU_DEPTH = 256
BF16_SUBLANES = 16
EXPERT_CHUNK = 128


def _cparams(*sem):
    return pltpu.CompilerParams(dimension_semantics=sem, vmem_limit_bytes=VMEM_LIMIT_BYTES)


def _tile(n, want):
    t = min(n, want)
    while n % t:
        t //= 2
    return t


def _dot(a, b):
    return jnp.dot(a, b, preferred_element_type=F32)


def _dot_nt(a, b):
    return lax.dot_general(a, b, (((1,), (1,)), ((), ())), preferred_element_type=F32)


def _in_proj_body(x_ref, pos_ref, inv_ref, w_ref, sbq, sbk, sbv, dfq, dfk, dfv, mxq):
    xb = x_ref[...].astype(BF16)
    ang = pos_ref[...].astype(F32) * inv_ref[...]
    d = lax.broadcasted_iota(jnp.int32, ang.shape, 1) & (HALF_COLS - 1)
    first = d < ROPE_DIM // 2
    rot = d < ROPE_DIM
    cos = jnp.where(rot, jnp.cos(ang), 1.0)
    sin = jnp.sin(ang)
    sin = jnp.where(first, -sin, jnp.where(rot, sin, 0.0))

    def proj(g):
        return _dot(xb, w_ref[:, g * WIDTH:(g + 1) * WIDTH])

    def rotary(p):
        outs = []
        for j in range(WIDTH // LANES):
            xj = p[:, j * LANES:(j + 1) * LANES]
            partner = jnp.where(first, pltpu.roll(xj, LANES - ROPE_DIM // 2, 1),
                                pltpu.roll(xj, ROPE_DIM // 2, 1))
            outs.append(xj * cos + partner * sin)
        return jnp.concatenate(outs, axis=1)

    scale = HALF_COLS ** -0.5
    sbq[...] = (proj(0) * scale).astype(BF16)
    sbk[...] = proj(1).astype(BF16)
    sbv[...] = proj(2).astype(BF16)
    dfq[...] = (rotary(proj(3)) * scale).astype(BF16)
    dfk[...] = rotary(proj(4)).astype(BF16)
    vt = proj(5).T
    for h in range(WIDTH // HEAD_COLS):
        dfv[0, h, :HEAD_COLS, :] = vt[h * HEAD_COLS:(h + 1) * HEAD_COLS].astype(BF16)
        dfv[0, h, HEAD_COLS:, :] = jnp.ones((DF_ONES_ROWS, vt.shape[1]), BF16)
    mxq[...] = proj(6).astype(BF16)


def _in_proj(x2, pos2, inv_lane, w_in, tm):
    T, D = x2.shape
    out = jax.ShapeDtypeStruct((T, WIDTH), BF16)
    blk = pl.BlockSpec((tm, WIDTH), lambda i: (i, 0))
    heads = WIDTH // HEAD_COLS
    vt_rows = HEAD_COLS + DF_ONES_ROWS
    vt_out = jax.ShapeDtypeStruct((T // tm, heads, vt_rows, tm), BF16)
    vt_blk = pl.BlockSpec((1, heads, vt_rows, tm), lambda i: (i, 0, 0, 0))
    return pl.pallas_call(
        _in_proj_body,
        grid=(T // tm,),
        in_specs=[pl.BlockSpec((tm, D), lambda i: (i, 0)),
                  pl.BlockSpec((tm, 1), lambda i: (i, 0)),
                  pl.BlockSpec((1, LANES), lambda i: (0, 0)),
                  pl.BlockSpec((D, N_GROUPS * WIDTH), lambda i: (0, 0))],
        out_specs=[blk] * 5 + [vt_blk, blk],
        out_shape=[out] * 5 + [vt_out, out],
        compiler_params=_cparams("parallel"),
        name="in_proj",
    )(x2, pos2, inv_lane, w_in)


def _sb_body(q_ref, k_ref, v_ref, o_ref, acc_ref, c_ref, *, tq):
    i = pl.program_id(2)
    qf = q_ref[...].astype(F32)
    lane = lax.broadcasted_iota(jnp.int32, qf.shape, 1)
    low = lane < HALF_COLS
    qs = (jnp.where(low, qf, 0.0).astype(BF16), jnp.where(low, 0.0, qf).astype(BF16))
    row = lax.broadcasted_iota(jnp.int32, (tq, tq), 0)
    col = lax.broadcasted_iota(jnp.int32, (tq, tq), 1)
    later = (row > col).astype(BF16)
    acc_ref[...] = jnp.zeros_like(acc_ref)
    c_ref[...] = jnp.zeros_like(c_ref)

    def cond(carry):
        kb, cmax = carry
        return jnp.logical_and(kb >= 0, cmax > SB_SKIP_BELOW)

    def body(carry):
        kb, _ = carry
        start = pl.multiple_of(kb * tq, tq)
        k = k_ref[pl.ds(start, tq), :]
        v = v_ref[pl.ds(start, tq), :]
        causal = (kb * tq + col) < (i * tq + row)
        cmax = None
        for h in range(2):
            z = _dot_nt(qs[h], k)
            t = jnp.log1p(jnp.exp(-jnp.abs(z)))
            log_beta = jnp.minimum(z, 0.0) - t
            lom = jnp.where(causal, -jnp.maximum(z, 0.0) - t, 0.0)
            hi = lom.astype(BF16)
            lo = (lom - hi.astype(F32)).astype(BF16)
            rest = _dot(hi, later) + _dot(lo, later)
            c = c_ref[h]
            a = jnp.where(causal, jnp.exp(log_beta + rest + c), 0.0)
            acc_ref[h] += _dot(a.astype(BF16), v)
            c_new = c + rest[:, 0:1] + lom[:, 0:1]
            c_ref[h] = c_new
            m = jnp.max(c_new)
            cmax = m if cmax is None else jnp.maximum(cmax, m)
        return kb - 1, cmax

    lax.while_loop(cond, body, (i, jnp.float32(0.0)))
    o_ref[...] = jnp.where(low, acc_ref[0], acc_ref[1]).astype(BF16)


def _sb_attention(q, k, v, B, S):
    T = q.shape[0]
    tq = _tile(S, 256)
    nq = S // tq
    k3 = k.reshape(B, S, WIDTH)
    v3 = v.reshape(B, S, WIDTH)
    kv_spec = pl.BlockSpec((None, S, HEAD_COLS), lambda b, p, i: (b, 0, p))
    return pl.pallas_call(
        functools.partial(_sb_body, tq=tq),
        grid=(B, WIDTH // HEAD_COLS, nq),
        in_specs=[pl.BlockSpec((tq, HEAD_COLS), lambda b, p, i: (b * nq + i, p)), kv_spec, kv_spec],
        out_specs=pl.BlockSpec((tq, HEAD_COLS), lambda b, p, i: (b * nq + i, p)),
        out_shape=jax.ShapeDtypeStruct((T, WIDTH), BF16),
        scratch_shapes=[pltpu.VMEM((2, tq, HEAD_COLS), F32), pltpu.VMEM((2, tq, 1), F32)],
        compiler_params=_cparams("parallel", "parallel", "arbitrary"),
        name="sb_attention",
    )(q, k3, v3)


def _df_body(lam_ref, g_ref, q_ref, k_ref, vt_ref, o_ref, m_ref, acc_ref, sa_ref, sb_ref, *, tq,
             lambda_init):
    i = pl.program_id(2)
    qt = q_ref[...].astype(F32).T
    sub = lax.broadcasted_iota(jnp.int32, qt.shape, 0)
    low = sub < HALF_COLS
    qts = (jnp.where(low, qt, 0.0).astype(BF16), jnp.where(low, 0.0, qt).astype(BF16))
    m_ref[...] = jnp.full_like(m_ref, -jnp.inf)
    acc_ref[...] = jnp.zeros_like(acc_ref)

    def scores(kb, s_ref):
        k = k_ref[pl.ds(pl.multiple_of(kb * tq, tq), tq), :]
        for m in range(2):
            s_ref[m] = _dot(k, qts[m])

    def consume(kb, s_ref, diagonal):
        vt = vt_ref[kb]
        for m in range(2):
            s = s_ref[m]
            if diagonal:
                key = lax.broadcasted_iota(jnp.int32, (tq, tq), 0)
                qry = lax.broadcasted_iota(jnp.int32, (tq, tq), 1)
                s = jnp.where(key <= qry, s, -jnp.inf)
            m_old = m_ref[m]
            m_new = jnp.maximum(m_old, jnp.max(s, axis=0, keepdims=True))
            p = jnp.exp(s - m_new).astype(BF16)
            acc_ref[m] = jnp.exp(m_old - m_new) * acc_ref[m] + _dot(vt, p)
            m_ref[m] = m_new

    scores(0, sa_ref)

    def pair(j, carry):
        scores(2 * j + 1, sb_ref)
        consume(2 * j, sa_ref, False)
        scores(2 * j + 2, sa_ref)
        consume(2 * j + 1, sb_ref, False)
        return carry

    lax.fori_loop(0, lax.shift_right_logical(i, 1), pair, 0)
    odd = (i & 1) == 1

    @pl.when(odd)
    def _():
        scores(i, sb_ref)
        consume(i - 1, sa_ref, False)
        consume(i, sb_ref, True)

    @pl.when(jnp.logical_not(odd))
    def _():
        consume(i, sa_ref, True)

    lp = lam_ref[...]
    lam = (jnp.exp(jnp.sum(lp[0:1] * lp[1:2], axis=1, keepdims=True))
           - jnp.exp(jnp.sum(lp[2:3] * lp[3:4], axis=1, keepdims=True)) + lambda_init)
    outs = [acc_ref[m, :HEAD_COLS, :] / acc_ref[m, HEAD_COLS:HEAD_COLS + 1, :] for m in range(2)]
    o = (outs[0] - lam * outs[1]).T
    o = o * lax.rsqrt(jnp.mean(jnp.square(o), axis=1, keepdims=True) + RMS_EPS)
    o_ref[...] = (o * g_ref[...] * (1.0 - lambda_init)).astype(BF16)


def _df_attention(q, k, vt, lam_p, norm_g, lambda_init, B, S, tq):
    T = q.shape[0]
    nq = S // tq
    k3 = k.reshape(B, S, WIDTH)
    vt_rows = vt.shape[2]
    return pl.pallas_call(
        functools.partial(_df_body, tq=tq, lambda_init=lambda_init),
        grid=(B, WIDTH // HEAD_COLS, nq),
        in_specs=[pl.BlockSpec(lam_p.shape, lambda b, h, i: (0, 0)),
                  pl.BlockSpec((1, HEAD_COLS), lambda b, h, i: (0, 0)),
                  pl.BlockSpec((tq, HEAD_COLS), lambda b, h, i: (b * nq + i, h)),
                  pl.BlockSpec((None, S, HEAD_COLS), lambda b, h, i: (b, 0, h)),
                  pl.BlockSpec((nq, None, vt_rows, tq), lambda b, h, i: (b, h, 0, 0))],
        out_specs=pl.BlockSpec((tq, HEAD_COLS), lambda b, h, i: (b * nq + i, h)),
        out_shape=jax.ShapeDtypeStruct((T, WIDTH), BF16),
        scratch_shapes=[pltpu.VMEM((2, 1, tq), F32), pltpu.VMEM((2, vt_rows, tq), F32),
                        pltpu.VMEM((2, tq, tq), F32), pltpu.VMEM((2, tq, tq), F32)],
        compiler_params=_cparams("parallel", "parallel", "arbitrary"),
        name="df_attention",
    )(lam_p, norm_g, q, k3, vt)


def _mem_kv_body(mem_ref, w_ref, k_ref, v_ref):
    kv = _dot(mem_ref[...].astype(BF16), w_ref[...])
    k_ref[...] = kv[:, :WIDTH].astype(BF16)
    v_ref[...] = kv[:, WIDTH:].astype(BF16)


def _mem_kv(mem2, w_kv):
    n, D = mem2.shape
    out = jax.ShapeDtypeStruct((n, WIDTH), BF16)
    return pl.pallas_call(
        _mem_kv_body,
        grid=(1,),
        in_specs=[pl.BlockSpec((n, D), lambda i: (0, 0)), pl.BlockSpec(w_kv.shape, lambda i: (0, 0))],
        out_specs=[pl.BlockSpec((n, WIDTH), lambda i: (0, 0))] * 2,
        out_shape=[out, out],
        compiler_params=_cparams("arbitrary"),
        name="mem_kv",
    )(mem2, w_kv)


def _mx_body(q_ref, k_ref, v_ref, o_ref):
    scale = MX_HEAD_DIM ** -0.5
    for h in range(WIDTH // MX_HEAD_DIM):
        cols = slice(h * MX_HEAD_DIM, (h + 1) * MX_HEAD_DIM)
        s = _dot_nt(q_ref[:, cols], k_ref[:, cols]) * scale
        p = jnp.exp(s - jnp.max(s, axis=1, keepdims=True))
        p = p / jnp.sum(p, axis=1, keepdims=True)
        o_ref[:, cols] = _dot(p.astype(BF16), v_ref[:, cols]).astype(BF16)


def _mx_attention(q, mem_k, mem_v, B, S, M):
    T = q.shape[0]
    tq = _tile(S, 512)
    nq = S // tq
    kv_spec = pl.BlockSpec((None, M, WIDTH), lambda i: (i // nq, 0, 0))
    return pl.pallas_call(
        _mx_body,
        grid=(T // tq,),
        in_specs=[pl.BlockSpec((tq, WIDTH), lambda i: (i, 0)), kv_spec, kv_spec],
        out_specs=pl.BlockSpec((tq, WIDTH), lambda i: (i, 0)),
        out_shape=jax.ShapeDtypeStruct((T, WIDTH), BF16),
        compiler_params=_cparams("parallel"),
        name="mx_attention",
    )(q, mem_k.reshape(B, M, WIDTH), mem_v.reshape(B, M, WIDTH))


def _layer_norm(y, g, b):
    mu = jnp.mean(y, axis=1, keepdims=True)
    yc = y - mu
    var = jnp.mean(jnp.square(yc), axis=1, keepdims=True)
    return yc * lax.rsqrt(var + LN_EPS) * g + b


def _merge_body(x_ref, osb_ref, odf_ref, omx_ref, wg_ref, bg_ref, wsb_ref, wdf_ref, wmx_ref, wo_ref,
                g_ref, b_ref, x1_ref, x1t_ref, *, alpha):
    x = x_ref[...]
    xb = x.astype(BF16)
    D = x.shape[1]
    merged = None
    for n, (o_ref, w_ref) in enumerate(((osb_ref, wsb_ref), (odf_ref, wdf_ref), (omx_ref, wmx_ref))):
        cols = slice(n * D, (n + 1) * D)
        gate = 1.0 / (1.0 + jnp.exp(-(_dot(xb, wg_ref[:, cols]) + bg_ref[:, cols])))
        term = gate * _dot(o_ref[...], w_ref[...])
        merged = term if merged is None else merged + term
    mix = _dot(merged.astype(BF16), wo_ref[...])
    x1 = _layer_norm(alpha * x + mix, g_ref[...], b_ref[...])
    x1_ref[...] = x1
    x1t_ref[...] = x1.T.astype(BF16)


def _merge(x2, o_sb, o_df, o_mx, w_gate, b_gate, w_sb, w_df, w_mx, w_out, g, b, alpha):
    T, D = x2.shape
    tm = _tile(T, 256)
    full = lambda a: pl.BlockSpec(a.shape, lambda i: (0, 0))
    tok = lambda w: pl.BlockSpec((tm, w), lambda i: (i, 0))
    return pl.pallas_call(
        functools.partial(_merge_body, alpha=alpha),
        grid=(T // tm,),
        in_specs=[tok(D), tok(WIDTH), tok(WIDTH), tok(WIDTH), full(w_gate), full(b_gate), full(w_sb),
                  full(w_df), full(w_mx), full(w_out), full(g), full(b)],
        out_specs=[tok(D), pl.BlockSpec((D, tm), lambda i: (0, i))],
        out_shape=[jax.ShapeDtypeStruct((T, D), F32), jax.ShapeDtypeStruct((D, T), BF16)],
        compiler_params=_cparams("parallel"),
        name="merge_ln",
    )(x2, o_sb, o_df, o_mx, w_gate, b_gate, w_sb, w_df, w_mx, w_out, g, b)


def _top16(s, tie_order):
    idx = lax.broadcasted_iota(jnp.int32, s.shape, 0).astype(F32)
    work = s
    rank = jnp.full(s.shape, float(PEER_TOPK), F32)
    vals = []
    for r in range(PEER_TOPK):
        m = jnp.max(work, axis=0, keepdims=True)
        if tie_order:
            first = jnp.min(jnp.where(work == m, idx, float(s.shape[0])), axis=0, keepdims=True)
            hit = idx == first
        else:
            hit = work == m
        rank = jnp.where(hit, float(r), rank)
        work = jnp.where(hit, -jnp.inf, work)
        vals.append(m)
    return rank, vals


def _pair_select(v0, v1, tie_order):
    K = PEER_TOPK
    G = K // 2
    t1 = jnp.concatenate(v1, axis=0)
    t0_tail = jnp.concatenate(v0[G:], axis=0)
    sub = lax.broadcasted_iota(jnp.int32, (G,) + t1.shape[1:], 0)
    subf = sub.astype(F32)
    vals = [v0[0] + t1[:G], v0[0] + t1[G:]]
    pos = [subf, subf + float(G)]
    for a in range(1, G):
        vals.append(jnp.where(sub < K // (a + 1), v0[a] + t1[:G], -jnp.inf))
        pos.append(subf + float(a * K))
    vals.append(t0_tail + v1[0])
    pos.append((subf + float(G)) * float(K))
    cand = jnp.concatenate(vals, axis=0)
    posc = jnp.concatenate(pos, axis=0)
    work = cand
    sel = jnp.zeros(cand.shape, F32)
    for _ in range(K):
        m = jnp.max(work, axis=0, keepdims=True)
        if tie_order:
            first = jnp.min(jnp.where(work == m, posc, float(K * K)), axis=0, keepdims=True)
            hit = posc == first
        else:
            hit = work == m
        sel = jnp.where(hit, 1.0, sel)
        work = jnp.where(hit, -jnp.inf, work)
    rows = [jnp.sum(sel[:K], axis=0, keepdims=True)]
    rows += [jnp.sum(sel[K + G * (a - 1):K + G * a], axis=0, keepdims=True) for a in range(1, G)]
    nb = jnp.concatenate(rows + [sel[K + G * (G - 1):]], axis=0)
    z = jnp.sum(jnp.where(sel > 0.0, jnp.exp(cand - (v0[0] + v1[0])), 0.0), axis=0, keepdims=True)
    return nb, z, jnp.sum(sel, axis=0, keepdims=True)


def _route_body(xt_ref, wq_ref, keys_ref, a_ref, nbi_ref, bw_ref, r1_ref, q_scr):
    q_scr[...] = _dot(wq_ref[...], xt_ref[...]).astype(BF16)

    def head(h, carry):
        scores = []
        for p in range(2):
            hp = h * 2 + p
            q_hp = q_scr[pl.ds(pl.multiple_of(hp * PEER_HALF, PEER_HALF), PEER_HALF), :]
            scores.append(_dot(keys_ref[hp], q_hp))

        def select(tie_order):
            rank0, v0 = _top16(scores[0], tie_order)
            rank1, v1 = _top16(scores[1], tie_order)
            nb, z, n_pairs = _pair_select(v0, v1, tie_order)
            nbi = jnp.zeros_like(rank0)
            for a in range(PEER_TOPK):
                nbi = jnp.where(rank0 == float(a), nb[a:a + 1, :], nbi)
            keep0 = rank0 < float(PEER_TOPK)
            keep1 = rank1 < float(PEER_TOPK)
            a_ref[h] = jnp.where(keep0, jnp.exp(scores[0] - v0[0]), 0.0)
            nbi_ref[h] = nbi
            bw_ref[h] = jnp.where(keep1, jnp.exp(scores[1] - v1[0]) / z, 0.0).astype(BF16)
            r1_ref[h] = rank1.astype(BF16)
            n0 = jnp.sum(keep0.astype(F32), axis=0, keepdims=True)
            n1 = jnp.sum(keep1.astype(F32), axis=0, keepdims=True)
            k = float(PEER_TOPK)
            return jnp.max(jnp.abs(n0 - k) + jnp.abs(n1 - k) + jnp.abs(n_pairs - k))

        miscount = select(False)

        @pl.when(miscount > 0.0)
        def _():
            select(True)

        return carry

    lax.fori_loop(0, PEER_HEADS, head, 0)


def _route(x1t, wq_t, keys):
    D, T = x1t.shape
    tm = _tile(T, 256)
    out = jax.ShapeDtypeStruct((PEER_HEADS, PEER_NKEYS, T), F32)
    out_b = jax.ShapeDtypeStruct((PEER_HEADS, PEER_NKEYS, T), BF16)
    blk = pl.BlockSpec((PEER_HEADS, PEER_NKEYS, tm), lambda i: (0, 0, i))
    return pl.pallas_call(
        _route_body,
        grid=(T // tm,),
        in_specs=[pl.BlockSpec((D, tm), lambda i: (0, i)),
                  pl.BlockSpec(wq_t.shape, lambda i: (0, 0)),
                  pl.BlockSpec(keys.shape, lambda i: (0, 0, 0))],
        out_specs=[blk] * 4,
        out_shape=[out, out, out_b, out_b],
        scratch_shapes=[pltpu.VMEM((wq_t.shape[0], tm), BF16)],
        compiler_params=_cparams("parallel"),
        name="peer_route",
    )(x1t, wq_t, keys)


def _tables_body(u_ref, v_ref, ub_ref, vt_ref):
    ub_ref[...] = u_ref[...].astype(BF16)
    vt_ref[...] = v_ref[...].T.astype(BF16)


def _tables(u_tab, v_tab):
    E, D = u_tab.shape
    te = _tile(E, 512)
    return pl.pallas_call(
        _tables_body,
        grid=(E // te,),
        in_specs=[pl.BlockSpec((te, D), lambda e: (e, 0))] * 2,
        out_specs=[pl.BlockSpec((te, D), lambda e: (e, 0)), pl.BlockSpec((D, te), lambda e: (0, e))],
        out_shape=[jax.ShapeDtypeStruct((E, D), BF16), jax.ShapeDtypeStruct((D, E), BF16)],
        compiler_params=_cparams("parallel"),
        name="peer_tables",
    )(u_tab, v_tab)


def _token_row_bf16(ref, hd, i):
    tile = jnp.broadcast_to(ref[hd, pl.ds(i, 1), :], (BF16_SUBLANES, ref.shape[2])).astype(BF16)
    return jnp.concatenate([tile] * (PEER_NKEYS // BF16_SUBLANES), axis=0)


def _experts_body(xt_ref, u_ref, vt_ref, a_ref, nbi_ref, bw_ref, r1_ref, x1_ref, g_ref, b_ref, o_ref,
                  acc_ref, *, alpha, rows_per_step):
    e = pl.program_id(1)

    @pl.when(e == 0)
    def _():
        acc_ref[...] = jnp.zeros_like(acc_ref)

    xt = xt_ref[...]
    acts = []
    chunk = EXPERT_CHUNK
    for c in range(rows_per_step * PEER_NKEYS // chunk):
        hid_c = _dot(u_ref[c * chunk:(c + 1) * chunk, :], xt)
        for r in range(chunk // PEER_NKEYS):
            i = e * rows_per_step + c * (chunk // PEER_NKEYS) + r
            hid = hid_c[r * PEER_NKEYS:(r + 1) * PEER_NKEYS]
            gate = None
            for hd in range(PEER_HEADS):
                a = _token_row_bf16(a_ref, hd, i)
                nbi = _token_row_bf16(nbi_ref, hd, i)
                term = a * jnp.where(r1_ref[hd] < nbi, bw_ref[hd], jnp.zeros((), BF16))
                gate = term if gate is None else gate + term
            gelu = 0.5 * hid * (1.0 + lax.erf(hid * (2.0 ** -0.5)))
            acts.append(gelu.astype(BF16) * gate)
    acc_ref[...] += _dot(vt_ref[...], jnp.concatenate(acts, axis=0))

    @pl.when(e == pl.num_programs(1) - 1)
    def _():
        y = alpha * x1_ref[...] + acc_ref[...].T
        o_ref[...] = _layer_norm(y, g_ref[...], b_ref[...])


def _experts(x1, x1t, u_b, v_t, a, nbi, bw, r1, g, b, alpha):
    T, D = x1.shape
    E = u_b.shape[0]
    tm = _tile(T, 512)
    te = _tile(E, 1024)
    route_spec = pl.BlockSpec((PEER_HEADS, PEER_NKEYS, tm), lambda t, e: (0, 0, t))
    vec = lambda arr: pl.BlockSpec(arr.shape, lambda t, e: (0, 0))
    return pl.pallas_call(
        functools.partial(_experts_body, alpha=alpha, rows_per_step=te // PEER_NKEYS),
        grid=(T // tm, E // te),
        in_specs=[pl.BlockSpec((D, tm), lambda t, e: (0, t)),
                  pl.BlockSpec((te, D), lambda t, e: (e, 0)),
                  pl.BlockSpec((D, te), lambda t, e: (0, e)),
                  route_spec, route_spec, route_spec, route_spec,
                  pl.BlockSpec((tm, D), lambda t, e: (t, 0)), vec(g), vec(b)],
        out_specs=pl.BlockSpec((tm, D), lambda t, e: (t, 0)),
        out_shape=jax.ShapeDtypeStruct((T, D), F32),
        scratch_shapes=[pltpu.VMEM((D, tm), F32)],
        compiler_params=_cparams("parallel", "arbitrary"),
        name="peer_experts",
    )(x1t, u_b, v_t, a, nbi, bw, r1, x1, g, b)


def kernel(x, mem, positions, w_in, w_mem_kv, df_lambda, df_norm_g, w_gate, b_gate, w_br_sb, w_br_df,
           w_br_mx, w_out, ln1_g, ln1_b, w_query, sub_keys, u_tab, v_tab, ln2_g, ln2_b):
    B, S, D = x.shape
    M = mem.shape[1]
    depth = w_in.shape[0]
    T = B * S
    alpha = (2.0 * depth) ** 0.25

    inv = ROPE_THETA ** (-jnp.arange(0, ROPE_DIM, 2, dtype=F32) / ROPE_DIM)
    d = jnp.arange(LANES) % HALF_COLS
    inv_lane = jnp.where(d < ROPE_DIM, inv[d % (ROPE_DIM // 2)], 0.0).reshape(1, LANES).astype(F32)
    pos2 = positions.reshape(T, 1)
    mem2 = mem.reshape(B * M, D)
    x2 = x.reshape(T, D)
    row = lambda v: v.reshape(1, -1)

    for l in range(depth):
        lambda_init = 0.8 - 0.6 * math.exp(-0.3 * l)
        t_df = _tile(S, 512)
        sbq, sbk, sbv, dfq, dfk, dfvt, mxq = _in_proj(x2, pos2, inv_lane, w_in[l].astype(BF16), t_df)
        o_sb = _sb_attention(sbq, sbk, sbv, B, S)
        o_df = _df_attention(dfq, dfk, dfvt, df_lambda[l], row(df_norm_g[l]), lambda_init, B, S, t_df)
        mem_k, mem_v = _mem_kv(mem2, w_mem_kv[l].astype(BF16))
        o_mx = _mx_attention(mxq, mem_k, mem_v, B, S, M)
        x1, x1t = _merge(x2, o_sb, o_df, o_mx, w_gate[l].astype(BF16), row(b_gate[l]),
                         w_br_sb[l].astype(BF16), w_br_df[l].astype(BF16), w_br_mx[l].astype(BF16),
                         w_out[l].astype(BF16), row(ln1_g[l]), row(ln1_b[l]), alpha)
        keys = sub_keys[l].reshape(PEER_HEADS * 2, PEER_NKEYS, PEER_HALF).astype(BF16)
        a, nbi, bw, r1 = _route(x1t, w_query[l].T.astype(BF16), keys)
        u_b, v_t = _tables(u_tab[l], v_tab[l])
        x2 = _experts(x1, x1t, u_b, v_t, a, nbi, bw, r1, row(ln2_g[l]), row(ln2_b[l]), alpha)
    return x2.reshape(B, S, D)
```

```python
import functools
import math

import jax
import jax.numpy as jnp
from jax import lax
from jax.experimental import pallas as pl
from jax.experimental.pallas import tpu as pltpu

F32 = jnp.float32
BF16 = jnp.bfloat16

WIDTH = 512
N_GROUPS = 7
HEAD_COLS = 128
HALF_COLS = 64
ROPE_THETA = 500000.0
ROPE_DIM = 16
MX_HEAD_DIM = 128
DF_ONES_ROWS = 16
PEER_HEADS = 8
PEER_NKEYS = 128
PEER_TOPK = 16
PEER_HALF = 128
LN_EPS = 1e-5
RMS_EPS = 1e-5
SB_SKIP_BELOW = -110.0

VMEM_LIMIT_BYTES = 56 * 1024 * 1024
LANES = 128
MXU_DEPTH = 256
BF16_SUBLANES = 16
EXPERT_CHUNK = 128


def _cparams(*sem):
    return pltpu.CompilerParams(dimension_semantics=sem, vmem_limit_bytes=VMEM_LIMIT_BYTES)


def _tile(n, want):
    t = min(n, want)
    while n % t:
        t //= 2
    return t


def _dot(a, b):
    return jnp.dot(a, b, preferred_element_type=F32)


def _dot_nt(a, b):
    return lax.dot_general(a, b, (((1,), (1,)), ((), ())), preferred_element_type=F32)


def _in_proj_body(x_ref, pos_ref, inv_ref, w_ref, sbq, sbk, sbv, dfq, dfk, dfv, mxq):
    xb = x_ref[...].astype(BF16)
    ang = pos_ref[...].astype(F32) * inv_ref[...]
    d = lax.broadcasted_iota(jnp.int32, ang.shape, 1) & (HALF_COLS - 1)
    first = d < ROPE_DIM // 2
    rot = d < ROPE_DIM
    cos = jnp.where(rot, jnp.cos(ang), 1.0)
    sin = jnp.sin(ang)
    sin = jnp.where(first, -sin, jnp.where(rot, sin, 0.0))

    def proj(g):
        return _dot(xb, w_ref[:, g * WIDTH:(g + 1) * WIDTH])

    def rotary(p):
        outs = []
        for j in range(WIDTH // LANES):
            xj = p[:, j * LANES:(j + 1) * LANES]
            partner = jnp.where(first, pltpu.roll(xj, LANES - ROPE_DIM // 2, 1),
                                pltpu.roll(xj, ROPE_DIM // 2, 1))
            outs.append(xj * cos + partner * sin)
        return jnp.concatenate(outs, axis=1)

    scale = HALF_COLS ** -0.5
    sbq[...] = (proj(0) * scale).astype(BF16)
    sbk[...] = proj(1).astype(BF16)
    sbv[...] = proj(2).astype(BF16)
    dfq[...] = (rotary(proj(3)) * scale).astype(BF16)
    dfk[...] = rotary(proj(4)).astype(BF16)
    vt = proj(5).T
    for h in range(WIDTH // HEAD_COLS):
        dfv[0, h, :HEAD_COLS, :] = vt[h * HEAD_COLS:(h + 1) * HEAD_COLS].astype(BF16)
        dfv[0, h, HEAD_COLS:, :] = jnp.ones((DF_ONES_ROWS, vt.shape[1]), BF16)
    mxq[...] = proj(6).astype(BF16)


def _in_proj(x2, pos2, inv_lane, w_in, tm):
    T, D = x2.shape
    out = jax.ShapeDtypeStruct((T, WIDTH), BF16)
    blk = pl.BlockSpec((tm, WIDTH), lambda i: (i, 0))
    heads = WIDTH // HEAD_COLS
    vt_rows = HEAD_COLS + DF_ONES_ROWS
    vt_out = jax.ShapeDtypeStruct((T // tm, heads, vt_rows, tm), BF16)
    vt_blk = pl.BlockSpec((1, heads, vt_rows, tm), lambda i: (i, 0, 0, 0))
    return pl.pallas_call(
        _in_proj_body,
        grid=(T // tm,),
        in_specs=[pl.BlockSpec((tm, D), lambda i: (i, 0)),
                  pl.BlockSpec((tm, 1), lambda i: (i, 0)),
                  pl.BlockSpec((1, LANES), lambda i: (0, 0)),
                  pl.BlockSpec((D, N_GROUPS * WIDTH), lambda i: (0, 0))],
        out_specs=[blk] * 5 + [vt_blk, blk],
        out_shape=[out] * 5 + [vt_out, out],
        compiler_params=_cparams("parallel"),
        name="in_proj",
    )(x2, pos2, inv_lane, w_in)


def _sb_body(q_ref, k_ref, v_ref, o_ref, acc_ref, c_ref, *, tq):
    i = pl.program_id(2)
    qf = q_ref[...].astype(F32)
    lane = lax.broadcasted_iota(jnp.int32, qf.shape, 1)
    low = lane < HALF_COLS
    qs = (jnp.where(low, qf, 0.0).astype(BF16), jnp.where(low, 0.0, qf).astype(BF16))
    row = lax.broadcasted_iota(jnp.int32, (tq, tq), 0)
    col = lax.broadcasted_iota(jnp.int32, (tq, tq), 1)
    later = (row > col).astype(BF16)
    acc_ref[...] = jnp.zeros_like(acc_ref)
    c_ref[...] = jnp.zeros_like(c_ref)

    def block(kb, diagonal):
        start = pl.multiple_of(kb * tq, tq)
        k = k_ref[pl.ds(start, tq), :]
        v = v_ref[pl.ds(start, tq), :]
        cmax = None
        for h in range(2):
            z = _dot_nt(qs[h], k)
            t = jnp.log(1.0 + jnp.exp(-jnp.abs(z)))
            log_beta = jnp.minimum(z, 0.0) - t
            lom = -jnp.maximum(z, 0.0) - t
            if diagonal:
                lom = jnp.where(col < row, lom, 0.0)
            hi = lom.astype(BF16)
            lo = (lom - hi.astype(F32)).astype(BF16)
            rest = _dot(hi, later) + _dot(lo, later)
            c = c_ref[h]
            a = jnp.exp(log_beta + rest + c)
            if diagonal:
                a = jnp.where(col < row, a, 0.0)
            acc_ref[h] += _dot(a.astype(BF16), v)
            c_new = c + rest[:, 0:1] + lom[:, 0:1]
            c_ref[h] = c_new
            m = jnp.max(c_new)
            cmax = m if cmax is None else jnp.maximum(cmax, m)
        return cmax

    def cond(carry):
        kb, cmax = carry
        return jnp.logical_and(kb >= 0, cmax > SB_SKIP_BELOW)

    def body(carry):
        return carry[0] - 1, block(carry[0], False)

    lax.while_loop(cond, body, (i - 1, block(i, True)))
    o_ref[...] = jnp.where(low, acc_ref[0], acc_ref[1]).astype(BF16)


def _sb_attention(q, k, v, B, S):
    T = q.shape[0]
    tq = _tile(S, 256)
    nq = S // tq
    k3 = k.reshape(B, S, WIDTH)
    v3 = v.reshape(B, S, WIDTH)
    kv_spec = pl.BlockSpec((None, S, HEAD_COLS), lambda b, p, i: (b, 0, p))
    return pl.pallas_call(
        functools.partial(_sb_body, tq=tq),
        grid=(B, WIDTH // HEAD_COLS, nq),
        in_specs=[pl.BlockSpec((tq, HEAD_COLS), lambda b, p, i: (b * nq + i, p)), kv_spec, kv_spec],
        out_specs=pl.BlockSpec((tq, HEAD_COLS), lambda b, p, i: (b * nq + i, p)),
        out_shape=jax.ShapeDtypeStruct((T, WIDTH), BF16),
        scratch_shapes=[pltpu.VMEM((2, tq, HEAD_COLS), F32), pltpu.VMEM((2, tq, 1), F32)],
        compiler_params=_cparams("parallel", "parallel", "arbitrary"),
        name="sb_attention",
    )(q, k3, v3)


def _df_body(lam_ref, g_ref, q_ref, k_ref, vt_ref, o_ref, m_ref, acc_ref, sa_ref, sb_ref, *, tq,
             lambda_init):
    i = pl.program_id(2)
    qt = q_ref[...].astype(F32).T
    sub = lax.broadcasted_iota(jnp.int32, qt.shape, 0)
    low = sub < HALF_COLS
    qts = (jnp.where(low, qt, 0.0).astype(BF16), jnp.where(low, 0.0, qt).astype(BF16))
    m_ref[...] = jnp.full_like(m_ref, -jnp.inf)
    acc_ref[...] = jnp.zeros_like(acc_ref)

    def scores(kb, s_ref):
        k = k_ref[pl.ds(pl.multiple_of(kb * tq, tq), tq), :]
        for m in range(2):
            s_ref[m] = _dot(k, qts[m])

    def consume(kb, s_ref, diagonal):
        vt = vt_ref[kb]
        for m in range(2):
            s = s_ref[m]
            if diagonal:
                key = lax.broadcasted_iota(jnp.int32, (tq, tq), 0)
                qry = lax.broadcasted_iota(jnp.int32, (tq, tq), 1)
                s = jnp.where(key <= qry, s, -jnp.inf)
            m_old = m_ref[m]
            m_new = jnp.maximum(m_old, jnp.max(s, axis=0, keepdims=True))
            p = jnp.exp(s - m_new).astype(BF16)
            acc_ref[m] = jnp.exp(m_old - m_new) * acc_ref[m] + _dot(vt, p)
            m_ref[m] = m_new

    scores(0, sa_ref)

    def pair(j, carry):
        scores(2 * j + 1, sb_ref)
        consume(2 * j, sa_ref, False)
        scores(2 * j + 2, sa_ref)
        consume(2 * j + 1, sb_ref, False)
        return carry

    lax.fori_loop(0, lax.shift_right_logical(i, 1), pair, 0)
    odd = (i & 1) == 1

    @pl.when(odd)
    def _():
        scores(i, sb_ref)
        consume(i - 1, sa_ref, False)
        consume(i, sb_ref, True)

    @pl.when(jnp.logical_not(odd))
    def _():
        consume(i, sa_ref, True)

    lp = lam_ref[...]
    lam = (jnp.exp(jnp.sum(lp[0:1] * lp[1:2], axis=1, keepdims=True))
           - jnp.exp(jnp.sum(lp[2:3] * lp[3:4], axis=1, keepdims=True)) + lambda_init)
    outs = [acc_ref[m, :HEAD_COLS, :] / acc_ref[m, HEAD_COLS:HEAD_COLS + 1, :] for m in range(2)]
    o = (outs[0] - lam * outs[1]).T
    o = o * lax.rsqrt(jnp.mean(jnp.square(o), axis=1, keepdims=True) + RMS_EPS)
    o_ref[...] = (o * g_ref[...] * (1.0 - lambda_init)).astype(BF16)


def _df_attention(q, k, vt, lam_p, norm_g, lambda_init, B, S, tq):
    T = q.shape[0]
    nq = S // tq
    k3 = k.reshape(B, S, WIDTH)
    vt_rows = vt.shape[2]
    return pl.pallas_call(
        functools.partial(_df_body, tq=tq, lambda_init=lambda_init),
        grid=(B, WIDTH // HEAD_COLS, nq),
        in_specs=[pl.BlockSpec(lam_p.shape, lambda b, h, i: (0, 0)),
                  pl.BlockSpec((1, HEAD_COLS), lambda b, h, i: (0, 0)),
                  pl.BlockSpec((tq, HEAD_COLS), lambda b, h, i: (b * nq + i, h)),
                  pl.BlockSpec((None, S, HEAD_COLS), lambda b, h, i: (b, 0, h)),
                  pl.BlockSpec((nq, None, vt_rows, tq), lambda b, h, i: (b, h, 0, 0))],
        out_specs=pl.BlockSpec((tq, HEAD_COLS), lambda b, h, i: (b * nq + i, h)),
        out_shape=jax.ShapeDtypeStruct((T, WIDTH), BF16),
        scratch_shapes=[pltpu.VMEM((2, 1, tq), F32), pltpu.VMEM((2, vt_rows, tq), F32),
                        pltpu.VMEM((2, tq, tq), F32), pltpu.VMEM((2, tq, tq), F32)],
        compiler_params=_cparams("parallel", "parallel", "arbitrary"),
        name="df_attention",
    )(lam_p, norm_g, q, k3, vt)


def _mem_kv_body(mem_ref, w_ref, k_ref, v_ref):
    kv = _dot(mem_ref[...].astype(BF16), w_ref[...])
    k_ref[...] = kv[:, :WIDTH].astype(BF16)
    v_ref[...] = kv[:, WIDTH:].astype(BF16)


def _mem_kv(mem2, w_kv):
    n, D = mem2.shape
    out = jax.ShapeDtypeStruct((n, WIDTH), BF16)
    return pl.pallas_call(
        _mem_kv_body,
        grid=(1,),
        in_specs=[pl.BlockSpec((n, D), lambda i: (0, 0)), pl.BlockSpec(w_kv.shape, lambda i: (0, 0))],
        out_specs=[pl.BlockSpec((n, WIDTH), lambda i: (0, 0))] * 2,
        out_shape=[out, out],
        compiler_params=_cparams("arbitrary"),
        name="mem_kv",
    )(mem2, w_kv)


def _mx_body(q_ref, k_ref, v_ref, o_ref):
    scale = MX_HEAD_DIM ** -0.5
    for h in range(WIDTH // MX_HEAD_DIM):
        cols = slice(h * MX_HEAD_DIM, (h + 1) * MX_HEAD_DIM)
        s = _dot_nt(q_ref[:, cols], k_ref[:, cols]) * scale
        p = jnp.exp(s - jnp.max(s, axis=1, keepdims=True))
        p = p / jnp.sum(p, axis=1, keepdims=True)
        o_ref[:, cols] = _dot(p.astype(BF16), v_ref[:, cols]).astype(BF16)


def _mx_attention(q, mem_k, mem_v, B, S, M):
    T = q.shape[0]
    tq = _tile(S, 512)
    nq = S // tq
    kv_spec = pl.BlockSpec((None, M, WIDTH), lambda i: (i // nq, 0, 0))
    return pl.pallas_call(
        _mx_body,
        grid=(T // tq,),
        in_specs=[pl.BlockSpec((tq, WIDTH), lambda i: (i, 0)), kv_spec, kv_spec],
        out_specs=pl.BlockSpec((tq, WIDTH), lambda i: (i, 0)),
        out_shape=jax.ShapeDtypeStruct((T, WIDTH), BF16),
        compiler_params=_cparams("parallel"),
        name="mx_attention",
    )(q, mem_k.reshape(B, M, WIDTH), mem_v.reshape(B, M, WIDTH))


def _layer_norm(y, g, b):
    mu = jnp.mean(y, axis=1, keepdims=True)
    yc = y - mu
    var = jnp.mean(jnp.square(yc), axis=1, keepdims=True)
    return yc * lax.rsqrt(var + LN_EPS) * g + b


def _merge_body(x_ref, osb_ref, odf_ref, omx_ref, wg_ref, bg_ref, wsb_ref, wdf_ref, wmx_ref, wo_ref,
                g_ref, b_ref, x1_ref, x1t_ref, *, alpha):
    x = x_ref[...]
    xb = x.astype(BF16)
    D = x.shape[1]
    merged = None
    for n, (o_ref, w_ref) in enumerate(((osb_ref, wsb_ref), (odf_ref, wdf_ref), (omx_ref, wmx_ref))):
        cols = slice(n * D, (n + 1) * D)
        gate = 1.0 / (1.0 + jnp.exp(-(_dot(xb, wg_ref[:, cols]) + bg_ref[:, cols])))
        term = gate * _dot(o_ref[...], w_ref[...])
        merged = term if merged is None else merged + term
    mix = _dot(merged.astype(BF16), wo_ref[...])
    x1 = _layer_norm(alpha * x + mix, g_ref[...], b_ref[...])
    x1_ref[...] = x1
    x1t_ref[...] = x1.T.astype(BF16)


def _merge(x2, o_sb, o_df, o_mx, w_gate, b_gate, w_sb, w_df, w_mx, w_out, g, b, alpha):
    T, D = x2.shape
    tm = _tile(T, 256)
    full = lambda a: pl.BlockSpec(a.shape, lambda i: (0, 0))
    tok = lambda w: pl.BlockSpec((tm, w), lambda i: (i, 0))
    return pl.pallas_call(
        functools.partial(_merge_body, alpha=alpha),
        grid=(T // tm,),
        in_specs=[tok(D), tok(WIDTH), tok(WIDTH), tok(WIDTH), full(w_gate), full(b_gate), full(w_sb),
                  full(w_df), full(w_mx), full(w_out), full(g), full(b)],
        out_specs=[tok(D), pl.BlockSpec((D, tm), lambda i: (0, i))],
        out_shape=[jax.ShapeDtypeStruct((T, D), F32), jax.ShapeDtypeStruct((D, T), BF16)],
        compiler_params=_cparams("parallel"),
        name="merge_ln",
    )(x2, o_sb, o_df, o_mx, w_gate, b_gate, w_sb, w_df, w_mx, w_out, g, b)


def _top16(s, tie_order):
    idx = lax.broadcasted_iota(jnp.int32, s.shape, 0).astype(F32)
    work = s
    rank = jnp.full(s.shape, float(PEER_TOPK), F32)
    vals = []
    for r in range(PEER_TOPK):
        m = jnp.max(work, axis=0, keepdims=True)
        if tie_order:
            first = jnp.min(jnp.where(work == m, idx, float(s.shape[0])), axis=0, keepdims=True)
            hit = idx == first
        else:
            hit = work == m
        rank = jnp.where(hit, float(r), rank)
        work = jnp.where(hit, -jnp.inf, work)
        vals.append(m)
    return rank, vals


def _pair_select(v0, v1, tie_order):
    K = PEER_TOPK
    G = K // 2
    t1 = jnp.concatenate(v1, axis=0)
    t0_tail = jnp.concatenate(v0[G:], axis=0)
    sub = lax.broadcasted_iota(jnp.int32, (G,) + t1.shape[1:], 0)
    subf = sub.astype(F32)
    vals = [v0[0] + t1[:G], v0[0] + t1[G:]]
    pos = [subf, subf + float(G)]
    for a in range(1, G):
        vals.append(jnp.where(sub < K // (a + 1), v0[a] + t1[:G], -jnp.inf))
        pos.append(subf + float(a * K))
    vals.append(t0_tail + v1[0])
    pos.append((subf + float(G)) * float(K))
    cand = jnp.concatenate(vals, axis=0)
    posc = jnp.concatenate(pos, axis=0)
    work = cand
    sel = jnp.zeros(cand.shape, F32)
    for _ in range(K):
        m = jnp.max(work, axis=0, keepdims=True)
        if tie_order:
            first = jnp.min(jnp.where(work == m, posc, float(K * K)), axis=0, keepdims=True)
            hit = posc == first
        else:
            hit = work == m
        sel = jnp.where(hit, 1.0, sel)
        work = jnp.where(hit, -jnp.inf, work)
    rows = [jnp.sum(sel[:K], axis=0, keepdims=True)]
    rows += [jnp.sum(sel[K + G * (a - 1):K + G * a], axis=0, keepdims=True) for a in range(1, G)]
    nb = jnp.concatenate(rows + [sel[K + G * (G - 1):]], axis=0)
    z = jnp.sum(jnp.where(sel > 0.0, jnp.exp(cand - (v0[0] + v1[0])), 0.0), axis=0, keepdims=True)
    return nb, z, jnp.sum(sel, axis=0, keepdims=True)


def _route_body(xt_ref, wq_ref, keys_ref, a_ref, nbi_ref, bw_ref, r1_ref, q_scr):
    q_scr[...] = _dot(wq_ref[...], xt_ref[...]).astype(BF16)

    def head(h, carry):
        scores = []
        for p in range(2):
            hp = h * 2 + p
            q_hp = q_scr[pl.ds(pl.multiple_of(hp * PEER_HALF, PEER_HALF), PEER_HALF), :]
            scores.append(_dot(keys_ref[hp], q_hp))

        def select(tie_order):
            rank0, v0 = _top16(scores[0], tie_order)
            rank1, v1 = _top16(scores[1], tie_order)
            nb, z, n_pairs = _pair_select(v0, v1, tie_order)
            nbi = jnp.zeros_like(rank0)
            for a in range(PEER_TOPK):
                nbi = jnp.where(rank0 == float(a), nb[a:a + 1, :], nbi)
            keep0 = rank0 < float(PEER_TOPK)
            keep1 = rank1 < float(PEER_TOPK)
            a_ref[h] = jnp.where(keep0, 0.5 * jnp.exp(scores[0] - v0[0]), 0.0)
            nbi_ref[h] = nbi
            bw_ref[h] = jnp.where(keep1, jnp.exp(scores[1] - v1[0]) / z, 0.0).astype(BF16)
            r1_ref[h] = rank1.astype(BF16)
            n0 = jnp.sum(keep0.astype(F32), axis=0, keepdims=True)
            n1 = jnp.sum(keep1.astype(F32), axis=0, keepdims=True)
            k = float(PEER_TOPK)
            return jnp.max(jnp.abs(n0 - k) + jnp.abs(n1 - k) + jnp.abs(n_pairs - k))

        miscount = select(False)

        @pl.when(miscount > 0.0)
        def _():
            select(True)

        return carry

    lax.fori_loop(0, PEER_HEADS, head, 0)


def _route(x1t, wq_t, keys):
    D, T = x1t.shape
    tm = _tile(T, 256)
    out = jax.ShapeDtypeStruct((PEER_HEADS, PEER_NKEYS, T), F32)
    out_b = jax.ShapeDtypeStruct((PEER_HEADS, PEER_NKEYS, T), BF16)
    blk = pl.BlockSpec((PEER_HEADS, PEER_NKEYS, tm), lambda i: (0, 0, i))
    return pl.pallas_call(
        _route_body,
        grid=(T // tm,),
        in_specs=[pl.BlockSpec((D, tm), lambda i: (0, i)),
                  pl.BlockSpec(wq_t.shape, lambda i: (0, 0)),
                  pl.BlockSpec(keys.shape, lambda i: (0, 0, 0))],
        out_specs=[blk] * 4,
        out_shape=[out, out, out_b, out_b],
        scratch_shapes=[pltpu.VMEM((wq_t.shape[0], tm), BF16)],
        compiler_params=_cparams("parallel"),
        name="peer_route",
    )(x1t, wq_t, keys)


def _tables_body(u_ref, v_ref, ub_ref, vt_ref):
    ub_ref[...] = u_ref[...].astype(BF16)
    vt_ref[...] = v_ref[...].T.astype(BF16)


def _tables(u_tab, v_tab):
    E, D = u_tab.shape
    te = _tile(E, 512)
    return pl.pallas_call(
        _tables_body,
        grid=(E // te,),
        in_specs=[pl.BlockSpec((te, D), lambda e: (e, 0))] * 2,
        out_specs=[pl.BlockSpec((te, D), lambda e: (e, 0)), pl.BlockSpec((D, te), lambda e: (0, e))],
        out_shape=[jax.ShapeDtypeStruct((E, D), BF16), jax.ShapeDtypeStruct((D, E), BF16)],
        compiler_params=_cparams("parallel"),
        name="peer_tables",
    )(u_tab, v_tab)


def _token_row_bf16(ref, hd, i):
    tile = jnp.broadcast_to(ref[hd, i:i + 1, :], (BF16_SUBLANES, ref.shape[2])).astype(BF16)
    return jnp.concatenate([tile] * (PEER_NKEYS // BF16_SUBLANES), axis=0)


def _experts_body(xt_ref, u_ref, vt_ref, a_ref, nbi_ref, bw_ref, r1_ref, x1_ref, g_ref, b_ref, o_ref,
                  acc_ref, *, alpha, rows_per_step):
    e = pl.program_id(1)

    @pl.when(e == 0)
    def _():
        acc_ref[...] = jnp.zeros_like(acc_ref)

    xt = xt_ref[...]
    acts = []
    chunk = EXPERT_CHUNK
    for c in range(rows_per_step * PEER_NKEYS // chunk):
        hid_c = _dot(u_ref[c * chunk:(c + 1) * chunk, :], xt)
        for r in range(chunk // PEER_NKEYS):
            i = c * (chunk // PEER_NKEYS) + r
            hid = hid_c[r * PEER_NKEYS:(r + 1) * PEER_NKEYS]
            gate = None
            for hd in range(PEER_HEADS):
                a = _token_row_bf16(a_ref, hd, i)
                nbi = _token_row_bf16(nbi_ref, hd, i)
                term = a * jnp.where(r1_ref[hd] < nbi, bw_ref[hd], jnp.zeros((), BF16))
                gate = term if gate is None else gate + term
            gelu2 = hid * (1.0 + lax.erf(hid * (2.0 ** -0.5)))
            acts.append(gelu2.astype(BF16) * gate)
    acc_ref[...] += _dot(vt_ref[...], jnp.concatenate(acts, axis=0))

    @pl.when(e == pl.num_programs(1) - 1)
    def _():
        y = alpha * x1_ref[...] + acc_ref[...].T
        o_ref[...] = _layer_norm(y, g_ref[...], b_ref[...])


def _experts(x1, x1t, u_b, v_t, a, nbi, bw, r1, g, b, alpha):
    T, D = x1.shape
    E = u_b.shape[0]
    tm = _tile(T, 512)
    te = _tile(E, 2048)
    route_spec = pl.BlockSpec((PEER_HEADS, PEER_NKEYS, tm), lambda t, e: (0, 0, t))
    row_spec = pl.BlockSpec((PEER_HEADS, te // PEER_NKEYS, tm), lambda t, e: (0, e, t))
    vec = lambda arr: pl.BlockSpec(arr.shape, lambda t, e: (0, 0))
    return pl.pallas_call(
        functools.partial(_experts_body, alpha=alpha, rows_per_step=te // PEER_NKEYS),
        grid=(T // tm, E // te),
        in_specs=[pl.BlockSpec((D, tm), lambda t, e: (0, t)),
                  pl.BlockSpec((te, D), lambda t, e: (e, 0)),
                  pl.BlockSpec((D, te), lambda t, e: (0, e)),
                  row_spec, row_spec, route_spec, route_spec,
                  pl.BlockSpec((tm, D), lambda t, e: (t, 0)), vec(g), vec(b)],
        out_specs=pl.BlockSpec((tm, D), lambda t, e: (t, 0)),
        out_shape=jax.ShapeDtypeStruct((T, D), F32),
        scratch_shapes=[pltpu.VMEM((D, tm), F32)],
        compiler_params=_cparams("parallel", "arbitrary"),
        name="peer_experts",
    )(x1t, u_b, v_t, a, nbi, bw, r1, x1, g, b)


def kernel(x, mem, positions, w_in, w_mem_kv, df_lambda, df_norm_g, w_gate, b_gate, w_br_sb, w_br_df,
           w_br_mx, w_out, ln1_g, ln1_b, w_query, sub_keys, u_tab, v_tab, ln2_g, ln2_b):
    B, S, D = x.shape
    M = mem.shape[1]
    depth = w_in.shape[0]
    T = B * S
    alpha = (2.0 * depth) ** 0.25

    inv = ROPE_THETA ** (-jnp.arange(0, ROPE_DIM, 2, dtype=F32) / ROPE_DIM)
    d = jnp.arange(LANES) % HALF_COLS
    inv_lane = jnp.where(d < ROPE_DIM, inv[d % (ROPE_DIM // 2)], 0.0).reshape(1, LANES).astype(F32)
    pos2 = positions.reshape(T, 1)
    mem2 = mem.reshape(B * M, D)
    x2 = x.reshape(T, D)
    row = lambda v: v.reshape(1, -1)

    for l in range(depth):
        lambda_init = 0.8 - 0.6 * math.exp(-0.3 * l)
        t_df = _tile(S, 512)
        sbq, sbk, sbv, dfq, dfk, dfvt, mxq = _in_proj(x2, pos2, inv_lane, w_in[l].astype(BF16), t_df)
        o_sb = _sb_attention(sbq, sbk, sbv, B, S)
        o_df = _df_attention(dfq, dfk, dfvt, df_lambda[l], row(df_norm_g[l]), lambda_init, B, S, t_df)
        mem_k, mem_v = _mem_kv(mem2, w_mem_kv[l].astype(BF16))
        o_mx = _mx_attention(mxq, mem_k, mem_v, B, S, M)
        x1, x1t = _merge(x2, o_sb, o_df, o_mx, w_gate[l].astype(BF16), row(b_gate[l]),
                         w_br_sb[l].astype(BF16), w_br_df[l].astype(BF16), w_br_mx[l].astype(BF16),
                         w_out[l].astype(BF16), row(ln1_g[l]), row(ln1_b[l]), alpha)
        keys = sub_keys[l].reshape(PEER_HEADS * 2, PEER_NKEYS, PEER_HALF).astype(BF16)
        a, nbi, bw, r1 = _route(x1t, w_query[l].T.astype(BF16), keys)
        u_b, v_t = _tables(u_tab[l], v_tab[l])
        x2 = _experts(x1, x1t, u_b, v_t, a, nbi, bw, r1, row(ln2_g[l]), row(ln2_b[l]), alpha)
    return x2.reshape(B, S, D)
```

```python
import functools
import math

import jax
import jax.numpy as jnp
from jax import lax
from jax.experimental import pallas as pl
from jax.experimental.pallas import tpu as pltpu

F32 = jnp.float32
BF16 = jnp.bfloat16

WIDTH = 512
N_GROUPS = 7
HEAD_COLS = 128
HALF_COLS = 64
ROPE_THETA = 500000.0
ROPE_DIM = 16
MX_HEAD_DIM = 128
DF_ONES_ROWS = 16
PEER_HEADS = 8
PEER_NKEYS = 128
PEER_TOPK = 16
PEER_HALF = 128
LN_EPS = 1e-5
RMS_EPS = 1e-5
SB_SKIP_BELOW = -110.0

VMEM_LIMIT_BYTES = 56 * 1024 * 1024
LANES = 128
MXU_DEPTH = 256
BF16_SUBLANES = 16
EXPERT_CHUNK = 128


def _cparams(*sem):
    return pltpu.CompilerParams(dimension_semantics=sem, vmem_limit_bytes=VMEM_LIMIT_BYTES)


def _tile(n, want):
    t = min(n, want)
    while n % t:
        t //= 2
    return t


def _dot(a, b):
    return jnp.dot(a, b, preferred_element_type=F32)


def _dot_nt(a, b):
    return lax.dot_general(a, b, (((1,), (1,)), ((), ())), preferred_element_type=F32)


def _in_proj_body(x_ref, pos_ref, inv_ref, w_ref, sbq, sbk, sbv, dfq, dfk, dfv, mxq):
    xb = x_ref[...].astype(BF16)
    ang = pos_ref[...].astype(F32) * inv_ref[...]
    d = lax.broadcasted_iota(jnp.int32, ang.shape, 1) & (HALF_COLS - 1)
    first = d < ROPE_DIM // 2
    rot = d < ROPE_DIM
    cos = jnp.where(rot, jnp.cos(ang), 1.0)
    sin = jnp.sin(ang)
    sin = jnp.where(first, -sin, jnp.where(rot, sin, 0.0))

    def proj(g):
        return _dot(xb, w_ref[:, g * WIDTH:(g + 1) * WIDTH])

    def rotary(p):
        outs = []
        for j in range(WIDTH // LANES):
            xj = p[:, j * LANES:(j + 1) * LANES]
            partner = jnp.where(first, pltpu.roll(xj, LANES - ROPE_DIM // 2, 1),
                                pltpu.roll(xj, ROPE_DIM // 2, 1))
            outs.append(xj * cos + partner * sin)
        return jnp.concatenate(outs, axis=1)

    scale = HALF_COLS ** -0.5
    sbq[...] = (proj(0) * scale).astype(BF16)
    sbk[...] = proj(1).astype(BF16)
    sbv[...] = proj(2).astype(BF16)
    dfq[...] = (rotary(proj(3)) * scale).astype(BF16)
    dfk[...] = rotary(proj(4)).astype(BF16)
    vt = proj(5).T
    for h in range(WIDTH // HEAD_COLS):
        dfv[0, h, :HEAD_COLS, :] = vt[h * HEAD_COLS:(h + 1) * HEAD_COLS].astype(BF16)
        dfv[0, h, HEAD_COLS:, :] = jnp.ones((DF_ONES_ROWS, vt.shape[1]), BF16)
    mxq[...] = proj(6).astype(BF16)


def _in_proj(x2, pos2, inv_lane, w_in, tm):
    T, D = x2.shape
    out = jax.ShapeDtypeStruct((T, WIDTH), BF16)
    blk = pl.BlockSpec((tm, WIDTH), lambda i: (i, 0))
    heads = WIDTH // HEAD_COLS
    vt_rows = HEAD_COLS + DF_ONES_ROWS
    vt_out = jax.ShapeDtypeStruct((T // tm, heads, vt_rows, tm), BF16)
    vt_blk = pl.BlockSpec((1, heads, vt_rows, tm), lambda i: (i, 0, 0, 0))
    return pl.pallas_call(
        _in_proj_body,
        grid=(T // tm,),
        in_specs=[pl.BlockSpec((tm, D), lambda i: (i, 0)),
                  pl.BlockSpec((tm, 1), lambda i: (i, 0)),
                  pl.BlockSpec((1, LANES), lambda i: (0, 0)),
                  pl.BlockSpec((D, N_GROUPS * WIDTH), lambda i: (0, 0))],
        out_specs=[blk] * 5 + [vt_blk, blk],
        out_shape=[out] * 5 + [vt_out, out],
        compiler_params=_cparams("parallel"),
        name="in_proj",
    )(x2, pos2, inv_lane, w_in)


def _sb_body(q_ref, k_ref, v_ref, o_ref, acc_ref, c_ref, *, tq):
    g = pl.program_id(2)
    lane = lax.broadcasted_iota(jnp.int32, (tq, HEAD_COLS), 1)
    low = lane < HALF_COLS
    qs = []
    for tile in range(2):
        qf = q_ref[tile * tq:(tile + 1) * tq, :].astype(F32)
        qs.append((jnp.where(low, qf, 0.0).astype(BF16), jnp.where(low, 0.0, qf).astype(BF16)))
    row = lax.broadcasted_iota(jnp.int32, (tq, tq), 0)
    col = lax.broadcasted_iota(jnp.int32, (tq, tq), 1)
    later = (row > col).astype(BF16)
    acc_ref[...] = jnp.zeros_like(acc_ref)
    c_ref[...] = jnp.zeros_like(c_ref)

    def block(tile, kb, diagonal):
        start = pl.multiple_of(kb * tq, tq)
        k = k_ref[pl.ds(start, tq), :]
        v = v_ref[pl.ds(start, tq), :]
        cmax = None
        for head in range(2):
            h = 2 * tile + head
            z = _dot_nt(qs[tile][head], k)
            t = jnp.log(1.0 + jnp.exp(-jnp.abs(z)))
            log_beta = jnp.minimum(z, 0.0) - t
            lom = -jnp.maximum(z, 0.0) - t
            if diagonal:
                lom = jnp.where(col < row, lom, 0.0)
            hi = lom.astype(BF16)
            lo = (lom - hi.astype(F32)).astype(BF16)
            rest = _dot(hi, later) + _dot(lo, later)
            c = c_ref[h]
            a = jnp.exp(log_beta + rest + c)
            if diagonal:
                a = jnp.where(col < row, a, 0.0)
            acc_ref[h] += _dot(a.astype(BF16), v)
            c_new = c + rest[:, 0:1] + lom[:, 0:1]
            c_ref[h] = c_new
            m = jnp.max(c_new)
            cmax = m if cmax is None else jnp.maximum(cmax, m)
        return cmax

    def cond(carry):
        r, c0, c1 = carry
        more = jnp.logical_or(c0 > SB_SKIP_BELOW, c1 > SB_SKIP_BELOW)
        return jnp.logical_and(2 * g - 1 - r >= 0, more)

    def body(carry):
        r = carry[0]
        kb = 2 * g - 1 - r
        return r + 1, block(0, kb, False), block(1, kb + 1, False)

    r, _, c1 = lax.while_loop(cond, body, (jnp.int32(0), block(0, 2 * g, True), block(1, 2 * g + 1, True)))

    @pl.when(jnp.logical_and(r == 2 * g, c1 > SB_SKIP_BELOW))
    def _():
        block(1, 0, False)

    for tile in range(2):
        o_ref[tile * tq:(tile + 1) * tq, :] = jnp.where(
            low, acc_ref[2 * tile], acc_ref[2 * tile + 1]).astype(BF16)


def _sb_attention(q, k, v, B, S):
    T = q.shape[0]
    tq = _tile(S // 2, 256)
    ng = S // (2 * tq)
    k3 = k.reshape(B, S, WIDTH)
    v3 = v.reshape(B, S, WIDTH)
    kv_spec = pl.BlockSpec((None, S, HEAD_COLS), lambda b, p, g: (b, 0, p))
    q_spec = pl.BlockSpec((2 * tq, HEAD_COLS), lambda b, p, g: (b * ng + g, p))
    return pl.pallas_call(
        functools.partial(_sb_body, tq=tq),
        grid=(B, WIDTH // HEAD_COLS, ng),
        in_specs=[q_spec, kv_spec, kv_spec],
        out_specs=q_spec,
        out_shape=jax.ShapeDtypeStruct((T, WIDTH), BF16),
        scratch_shapes=[pltpu.VMEM((4, tq, HEAD_COLS), F32), pltpu.VMEM((4, tq, 1), F32)],
        compiler_params=_cparams("parallel", "parallel", "arbitrary"),
        name="sb_attention",
    )(q, k3, v3)


def _df_body(lam_ref, g_ref, q_ref, k_ref, vt_ref, o_ref, m_ref, acc_ref, sa_ref, sb_ref, *, tq,
             lambda_init):
    i = pl.program_id(2)
    qt = q_ref[...].astype(F32).T
    sub = lax.broadcasted_iota(jnp.int32, qt.shape, 0)
    low = sub < HALF_COLS
    qts = (jnp.where(low, qt, 0.0).astype(BF16), jnp.where(low, 0.0, qt).astype(BF16))
    m_ref[...] = jnp.full_like(m_ref, -jnp.inf)
    acc_ref[...] = jnp.zeros_like(acc_ref)

    def scores(kb, s_ref):
        k = k_ref[pl.ds(pl.multiple_of(kb * tq, tq), tq), :]
        for m in range(2):
            s_ref[m] = _dot(k, qts[m])

    def consume(kb, s_ref, diagonal):
        vt = vt_ref[kb]
        for m in range(2):
            s = s_ref[m]
            if diagonal:
                key = lax.broadcasted_iota(jnp.int32, (tq, tq), 0)
                qry = lax.broadcasted_iota(jnp.int32, (tq, tq), 1)
                s = jnp.where(key <= qry, s, -jnp.inf)
            m_old = m_ref[m]
            m_new = jnp.maximum(m_old, jnp.max(s, axis=0, keepdims=True))
            p = jnp.exp(s - m_new).astype(BF16)
            acc_ref[m] = jnp.exp(m_old - m_new) * acc_ref[m] + _dot(vt, p)
            m_ref[m] = m_new

    scores(0, sa_ref)

    def pair(j, carry):
        scores(2 * j + 1, sb_ref)
        consume(2 * j, sa_ref, False)
        scores(2 * j + 2, sa_ref)
        consume(2 * j + 1, sb_ref, False)
        return carry

    lax.fori_loop(0, lax.shift_right_logical(i, 1), pair, 0)
    odd = (i & 1) == 1

    @pl.when(odd)
    def _():
        scores(i, sb_ref)
        consume(i - 1, sa_ref, False)
        consume(i, sb_ref, True)

    @pl.when(jnp.logical_not(odd))
    def _():
        consume(i, sa_ref, True)

    lp = lam_ref[...]
    lam = (jnp.exp(jnp.sum(lp[0:1] * lp[1:2], axis=1, keepdims=True))
           - jnp.exp(jnp.sum(lp[2:3] * lp[3:4], axis=1, keepdims=True)) + lambda_init)
    outs = [acc_ref[m, :HEAD_COLS, :] / acc_ref[m, HEAD_COLS:HEAD_COLS + 1, :] for m in range(2)]
    o = (outs[0] - lam * outs[1]).T
    o = o * lax.rsqrt(jnp.mean(jnp.square(o), axis=1, keepdims=True) + RMS_EPS)
    o_ref[...] = (o * g_ref[...] * (1.0 - lambda_init)).astype(BF16)


def _df_attention(q, k, vt, lam_p, norm_g, lambda_init, B, S, tq):
    T = q.shape[0]
    nq = S // tq
    k3 = k.reshape(B, S, WIDTH)
    vt_rows = vt.shape[2]
    return pl.pallas_call(
        functools.partial(_df_body, tq=tq, lambda_init=lambda_init),
        grid=(B, WIDTH // HEAD_COLS, nq),
        in_specs=[pl.BlockSpec(lam_p.shape, lambda b, h, i: (0, 0)),
                  pl.BlockSpec((1, HEAD_COLS), lambda b, h, i: (0, 0)),
                  pl.BlockSpec((tq, HEAD_COLS), lambda b, h, i: (b * nq + i, h)),
                  pl.BlockSpec((None, S, HEAD_COLS), lambda b, h, i: (b, 0, h)),
                  pl.BlockSpec((nq, None, vt_rows, tq), lambda b, h, i: (b, h, 0, 0))],
        out_specs=pl.BlockSpec((tq, HEAD_COLS), lambda b, h, i: (b * nq + i, h)),
        out_shape=jax.ShapeDtypeStruct((T, WIDTH), BF16),
        scratch_shapes=[pltpu.VMEM((2, 1, tq), F32), pltpu.VMEM((2, vt_rows, tq), F32),
                        pltpu.VMEM((2, tq, tq), F32), pltpu.VMEM((2, tq, tq), F32)],
        compiler_params=_cparams("parallel", "parallel", "arbitrary"),
        name="df_attention",
    )(lam_p, norm_g, q, k3, vt)


def _mem_kv_body(mem_ref, w_ref, k_ref, v_ref):
    kv = _dot(mem_ref[...].astype(BF16), w_ref[...])
    k_ref[...] = kv[:, :WIDTH].astype(BF16)
    v_ref[...] = kv[:, WIDTH:].astype(BF16)


def _mem_kv(mem2, w_kv):
    n, D = mem2.shape
    out = jax.ShapeDtypeStruct((n, WIDTH), BF16)
    return pl.pallas_call(
        _mem_kv_body,
        grid=(1,),
        in_specs=[pl.BlockSpec((n, D), lambda i: (0, 0)), pl.BlockSpec(w_kv.shape, lambda i: (0, 0))],
        out_specs=[pl.BlockSpec((n, WIDTH), lambda i: (0, 0))] * 2,
        out_shape=[out, out],
        compiler_params=_cparams("arbitrary"),
        name="mem_kv",
    )(mem2, w_kv)


def _mx_body(q_ref, k_ref, v_ref, o_ref):
    scale = MX_HEAD_DIM ** -0.5
    for h in range(WIDTH // MX_HEAD_DIM):
        cols = slice(h * MX_HEAD_DIM, (h + 1) * MX_HEAD_DIM)
        s = _dot_nt(q_ref[:, cols], k_ref[:, cols]) * scale
        p = jnp.exp(s - jnp.max(s, axis=1, keepdims=True))
        p = p / jnp.sum(p, axis=1, keepdims=True)
        o_ref[:, cols] = _dot(p.astype(BF16), v_ref[:, cols]).astype(BF16)


def _mx_attention(q, mem_k, mem_v, B, S, M):
    T = q.shape[0]
    tq = _tile(S, 512)
    nq = S // tq
    kv_spec = pl.BlockSpec((None, M, WIDTH), lambda i: (i // nq, 0, 0))
    return pl.pallas_call(
        _mx_body,
        grid=(T // tq,),
        in_specs=[pl.BlockSpec((tq, WIDTH), lambda i: (i, 0)), kv_spec, kv_spec],
        out_specs=pl.BlockSpec((tq, WIDTH), lambda i: (i, 0)),
        out_shape=jax.ShapeDtypeStruct((T, WIDTH), BF16),
        compiler_params=_cparams("parallel"),
        name="mx_attention",
    )(q, mem_k.reshape(B, M, WIDTH), mem_v.reshape(B, M, WIDTH))


def _layer_norm(y, g, b):
    mu = jnp.mean(y, axis=1, keepdims=True)
    yc = y - mu
    var = jnp.mean(jnp.square(yc), axis=1, keepdims=True)
    return yc * lax.rsqrt(var + LN_EPS) * g + b


def _merge_body(x_ref, osb_ref, odf_ref, omx_ref, wg_ref, bg_ref, wsb_ref, wdf_ref, wmx_ref, wo_ref,
                g_ref, b_ref, x1_ref, x1t_ref, *, alpha):
    x = x_ref[...]
    xb = x.astype(BF16)
    D = x.shape[1]
    merged = None
    for n, (o_ref, w_ref) in enumerate(((osb_ref, wsb_ref), (odf_ref, wdf_ref), (omx_ref, wmx_ref))):
        cols = slice(n * D, (n + 1) * D)
        gate = 1.0 / (1.0 + jnp.exp(-(_dot(xb, wg_ref[:, cols]) + bg_ref[:, cols])))
        term = gate * _dot(o_ref[...], w_ref[...])
        merged = term if merged is None else merged + term
    mix = _dot(merged.astype(BF16), wo_ref[...])
    x1 = _layer_norm(alpha * x + mix, g_ref[...], b_ref[...])
    x1_ref[...] = x1
    x1t_ref[...] = x1.T.astype(BF16)


def _merge(x2, o_sb, o_df, o_mx, w_gate, b_gate, w_sb, w_df, w_mx, w_out, g, b, alpha):
    T, D = x2.shape
    tm = _tile(T, 512)
    full = lambda a: pl.BlockSpec(a.shape, lambda i: (0, 0))
    tok = lambda w: pl.BlockSpec((tm, w), lambda i: (i, 0))
    return pl.pallas_call(
        functools.partial(_merge_body, alpha=alpha),
        grid=(T // tm,),
        in_specs=[tok(D), tok(WIDTH), tok(WIDTH), tok(WIDTH), full(w_gate), full(b_gate), full(w_sb),
                  full(w_df), full(w_mx), full(w_out), full(g), full(b)],
        out_specs=[tok(D), pl.BlockSpec((D, tm), lambda i: (0, i))],
        out_shape=[jax.ShapeDtypeStruct((T, D), F32), jax.ShapeDtypeStruct((D, T), BF16)],
        compiler_params=_cparams("parallel"),
        name="merge_ln",
    )(x2, o_sb, o_df, o_mx, w_gate, b_gate, w_sb, w_df, w_mx, w_out, g, b)


def _top16(s, tie_order):
    idx = lax.broadcasted_iota(jnp.int32, s.shape, 0).astype(F32)
    work = s
    rank = jnp.full(s.shape, float(PEER_TOPK), F32)
    vals = []
    for r in range(PEER_TOPK):
        m = jnp.max(work, axis=0, keepdims=True)
        if tie_order:
            first = jnp.min(jnp.where(work == m, idx, float(s.shape[0])), axis=0, keepdims=True)
            hit = idx == first
        else:
            hit = work == m
        rank = jnp.where(hit, float(r), rank)
        work = jnp.where(hit, -jnp.inf, work)
        vals.append(m)
    return rank, vals


def _pair_select(v0, v1, tie_order):
    K = PEER_TOPK
    G = K // 2
    t1 = jnp.concatenate(v1, axis=0)
    t0_tail = jnp.concatenate(v0[G:], axis=0)
    sub = lax.broadcasted_iota(jnp.int32, (G,) + t1.shape[1:], 0)
    subf = sub.astype(F32)
    vals = [v0[0] + t1[:G], v0[0] + t1[G:]]
    pos = [subf, subf + float(G)]
    for a in range(1, G):
        vals.append(jnp.where(sub < K // (a + 1), v0[a] + t1[:G], -jnp.inf))
        pos.append(subf + float(a * K))
    vals.append(t0_tail + v1[0])
    pos.append((subf + float(G)) * float(K))
    cand = jnp.concatenate(vals, axis=0)
    posc = jnp.concatenate(pos, axis=0)
    work = cand
    sel = jnp.zeros(cand.shape, F32)
    for _ in range(K):
        m = jnp.max(work, axis=0, keepdims=True)
        if tie_order:
            first = jnp.min(jnp.where(work == m, posc, float(K * K)), axis=0, keepdims=True)
            hit = posc == first
        else:
            hit = work == m
        sel = jnp.where(hit, 1.0, sel)
        work = jnp.where(hit, -jnp.inf, work)
    rows = [jnp.sum(sel[:K], axis=0, keepdims=True)]
    rows += [jnp.sum(sel[K + G * (a - 1):K + G * a], axis=0, keepdims=True) for a in range(1, G)]
    nb = jnp.concatenate(rows + [sel[K + G * (G - 1):]], axis=0)
    z = jnp.sum(jnp.where(sel > 0.0, jnp.exp(cand - (v0[0] + v1[0])), 0.0), axis=0, keepdims=True)
    return nb, z, jnp.sum(sel, axis=0, keepdims=True)


def _route_body(xt_ref, wq_ref, keys_ref, a_ref, nbi_ref, bw_ref, r1_ref, q_scr):
    q_scr[...] = _dot(wq_ref[...], xt_ref[...]).astype(BF16)

    def head(h, carry):
        scores = []
        for p in range(2):
            hp = h * 2 + p
            q_hp = q_scr[pl.ds(pl.multiple_of(hp * PEER_HALF, PEER_HALF), PEER_HALF), :]
            scores.append(_dot(keys_ref[hp], q_hp))

        def select(tie_order):
            rank0, v0 = _top16(scores[0], tie_order)
            rank1, v1 = _top16(scores[1], tie_order)
            nb, z, n_pairs = _pair_select(v0, v1, tie_order)
            nbi = jnp.zeros_like(rank0)
            for a in range(PEER_TOPK):
                nbi = jnp.where(rank0 == float(a), nb[a:a + 1, :], nbi)
            keep0 = rank0 < float(PEER_TOPK)
            keep1 = rank1 < float(PEER_TOPK)
            a_ref[h] = jnp.where(keep0, 0.5 * jnp.exp(scores[0] - v0[0]), 0.0)
            nbi_ref[h] = nbi
            bw_ref[h] = jnp.where(keep1, jnp.exp(scores[1] - v1[0]) / z, 0.0).astype(BF16)
            r1_ref[h] = rank1.astype(BF16)
            n0 = jnp.sum(keep0.astype(F32), axis=0, keepdims=True)
            n1 = jnp.sum(keep1.astype(F32), axis=0, keepdims=True)
            k = float(PEER_TOPK)
            return jnp.max(jnp.abs(n0 - k) + jnp.abs(n1 - k) + jnp.abs(n_pairs - k))

        miscount = select(False)

        @pl.when(miscount > 0.0)
        def _():
            select(True)

        return carry

    lax.fori_loop(0, PEER_HEADS, head, 0)


def _route(x1t, wq_t, keys):
    D, T = x1t.shape
    tm = _tile(T, 256)
    out = jax.ShapeDtypeStruct((PEER_HEADS, PEER_NKEYS, T), F32)
    out_b = jax.ShapeDtypeStruct((PEER_HEADS, PEER_NKEYS, T), BF16)
    blk = pl.BlockSpec((PEER_HEADS, PEER_NKEYS, tm), lambda i: (0, 0, i))
    return pl.pallas_call(
        _route_body,
        grid=(T // tm,),
        in_specs=[pl.BlockSpec((D, tm), lambda i: (0, i)),
                  pl.BlockSpec(wq_t.shape, lambda i: (0, 0)),
                  pl.BlockSpec(keys.shape, lambda i: (0, 0, 0))],
        out_specs=[blk] * 4,
        out_shape=[out, out, out_b, out_b],
        scratch_shapes=[pltpu.VMEM((wq_t.shape[0], tm), BF16)],
        compiler_params=_cparams("parallel"),
        name="peer_route",
    )(x1t, wq_t, keys)


def _tables_body(u_ref, v_ref, ub_ref, vt_ref):
    ub_ref[...] = u_ref[...].astype(BF16)
    vt_ref[...] = v_ref[...].T.astype(BF16)


def _tables(u_tab, v_tab):
    E, D = u_tab.shape
    te = _tile(E, 512)
    return pl.pallas_call(
        _tables_body,
        grid=(E // te,),
        in_specs=[pl.BlockSpec((te, D), lambda e: (e, 0))] * 2,
        out_specs=[pl.BlockSpec((te, D), lambda e: (e, 0)), pl.BlockSpec((D, te), lambda e: (0, e))],
        out_shape=[jax.ShapeDtypeStruct((E, D), BF16), jax.ShapeDtypeStruct((D, E), BF16)],
        compiler_params=_cparams("parallel"),
        name="peer_tables",
    )(u_tab, v_tab)


def _token_row_bf16(ref, hd, i):
    tile = jnp.broadcast_to(ref[hd, i:i + 1, :], (BF16_SUBLANES, ref.shape[2])).astype(BF16)
    return jnp.concatenate([tile] * (PEER_NKEYS // BF16_SUBLANES), axis=0)


def _experts_body(xt_ref, u_ref, vt_ref, a_ref, nbi_ref, bw_ref, r1_ref, x1_ref, g_ref, b_ref, o_ref,
                  acc_ref, *, alpha, rows_per_step):
    e = pl.program_id(1)

    @pl.when(e == 0)
    def _():
        acc_ref[...] = jnp.zeros_like(acc_ref)

    xt = xt_ref[...]
    acts = []
    chunk = EXPERT_CHUNK
    for c in range(rows_per_step * PEER_NKEYS // chunk):
        hid_c = _dot(u_ref[c * chunk:(c + 1) * chunk, :], xt)
        for r in range(chunk // PEER_NKEYS):
            i = c * (chunk // PEER_NKEYS) + r
            hid = hid_c[r * PEER_NKEYS:(r + 1) * PEER_NKEYS]
            gate = None
            for hd in range(PEER_HEADS):
                a = _token_row_bf16(a_ref, hd, i)
                nbi = _token_row_bf16(nbi_ref, hd, i)
                term = a * jnp.where(r1_ref[hd] < nbi, bw_ref[hd], jnp.zeros((), BF16))
                gate = term if gate is None else gate + term
            gelu2 = hid * (1.0 + lax.erf(hid * (2.0 ** -0.5)))
            acts.append(gelu2.astype(BF16) * gate)
    acc_ref[...] += _dot(vt_ref[...], jnp.concatenate(acts, axis=0))

    @pl.when(e == pl.num_programs(1) - 1)
    def _():
        y = alpha * x1_ref[...] + acc_ref[...].T
        o_ref[...] = _layer_norm(y, g_ref[...], b_ref[...])


def _experts(x1, x1t, u_b, v_t, a, nbi, bw, r1, g, b, alpha):
    T, D = x1.shape
    E = u_b.shape[0]
    tm = _tile(T, 512)
    te = _tile(E, 2048)
    route_spec = pl.BlockSpec((PEER_HEADS, PEER_NKEYS, tm), lambda t, e: (0, 0, t))
    row_spec = pl.BlockSpec((PEER_HEADS, te // PEER_NKEYS, tm), lambda t, e: (0, e, t))
    vec = lambda arr: pl.BlockSpec(arr.shape, lambda t, e: (0, 0))
    return pl.pallas_call(
        functools.partial(_experts_body, alpha=alpha, rows_per_step=te // PEER_NKEYS),
        grid=(T // tm, E // te),
        in_specs=[pl.BlockSpec((D, tm), lambda t, e: (0, t)),
                  pl.BlockSpec((te, D), lambda t, e: (e, 0)),
                  pl.BlockSpec((D, te), lambda t, e: (0, e)),
                  row_spec, row_spec, route_spec, route_spec,
                  pl.BlockSpec((tm, D), lambda t, e: (t, 0)), vec(g), vec(b)],
        out_specs=pl.BlockSpec((tm, D), lambda t, e: (t, 0)),
        out_shape=jax.ShapeDtypeStruct((T, D), F32),
        scratch_shapes=[pltpu.VMEM((D, tm), F32)],
        compiler_params=_cparams("parallel", "arbitrary"),
        name="peer_experts",
    )(x1t, u_b, v_t, a, nbi, bw, r1, x1, g, b)


def kernel(x, mem, positions, w_in, w_mem_kv, df_lambda, df_norm_g, w_gate, b_gate, w_br_sb, w_br_df,
           w_br_mx, w_out, ln1_g, ln1_b, w_query, sub_keys, u_tab, v_tab, ln2_g, ln2_b):
    B, S, D = x.shape
    M = mem.shape[1]
    depth = w_in.shape[0]
    T = B * S
    alpha = (2.0 * depth) ** 0.25

    inv = ROPE_THETA ** (-jnp.arange(0, ROPE_DIM, 2, dtype=F32) / ROPE_DIM)
    d = jnp.arange(LANES) % HALF_COLS
    inv_lane = jnp.where(d < ROPE_DIM, inv[d % (ROPE_DIM // 2)], 0.0).reshape(1, LANES).astype(F32)
    pos2 = positions.reshape(T, 1)
    mem2 = mem.reshape(B * M, D)
    x2 = x.reshape(T, D)
    row = lambda v: v.reshape(1, -1)

    for l in range(depth):
        lambda_init = 0.8 - 0.6 * math.exp(-0.3 * l)
        t_df = _tile(S, 512)
        sbq, sbk, sbv, dfq, dfk, dfvt, mxq = _in_proj(x2, pos2, inv_lane, w_in[l].astype(BF16), t_df)
        o_sb = _sb_attention(sbq, sbk, sbv, B, S)
        o_df = _df_attention(dfq, dfk, dfvt, df_lambda[l], row(df_norm_g[l]), lambda_init, B, S, t_df)
        mem_k, mem_v = _mem_kv(mem2, w_mem_kv[l].astype(BF16))
        o_mx = _mx_attention(mxq, mem_k, mem_v, B, S, M)
        x1, x1t = _merge(x2, o_sb, o_df, o_mx, w_gate[l].astype(BF16), row(b_gate[l]),
                         w_br_sb[l].astype(BF16), w_br_df[l].astype(BF16), w_br_mx[l].astype(BF16),
                         w_out[l].astype(BF16), row(ln1_g[l]), row(ln1_b[l]), alpha)
        keys = sub_keys[l].reshape(PEER_HEADS * 2, PEER_NKEYS, PEER_HALF).astype(BF16)
        a, nbi, bw, r1 = _route(x1t, w_query[l].T.astype(BF16), keys)
        u_b, v_t = _tables(u_tab[l], v_tab[l])
        x2 = _experts(x1, x1t, u_b, v_t, a, nbi, bw, r1, row(ln2_g[l]), row(ln2_b[l]), alpha)
    return x2.reshape(B, S, D)
```

```python
import functools
import math

import jax
import jax.numpy as jnp
from jax import lax
from jax.experimental import pallas as pl
from jax.experimental.pallas import tpu as pltpu

F32 = jnp.float32
BF16 = jnp.bfloat16

WIDTH = 512
N_GROUPS = 7
HEAD_COLS = 128
HALF_COLS = 64
ROPE_THETA = 500000.0
ROPE_DIM = 16
MX_HEAD_DIM = 128
DF_ONES_ROWS = 16
PEER_HEADS = 8
PEER_NKEYS = 128
PEER_TOPK = 16
PEER_HALF = 128
LN_EPS = 1e-5
RMS_EPS = 1e-5
SB_SKIP_BELOW = -110.0

VMEM_LIMIT_BYTES = 56 * 1024 * 1024
LANES = 128
BF16_SUBLANES = 16
EXPERT_CHUNK = 128


def _cparams(*sem):
    return pltpu.CompilerParams(dimension_semantics=sem, vmem_limit_bytes=VMEM_LIMIT_BYTES)


def _tile(n, want):
    t = min(n, want)
    while n % t:
        t //= 2
    return t


def _dot(a, b):
    return jnp.dot(a, b, preferred_element_type=F32)


def _dot_nt(a, b):
    return lax.dot_general(a, b, (((1,), (1,)), ((), ())), preferred_element_type=F32)


def _in_proj_body(x_ref, pos_ref, inv_ref, w_ref, mk_ref, mv_ref, sbq, sbk, sbv, dfq, dfk, dfv, omx):
    xb = x_ref[...].astype(BF16)
    ang = pos_ref[...].astype(F32) * inv_ref[...]
    d = lax.broadcasted_iota(jnp.int32, ang.shape, 1) & (HALF_COLS - 1)
    first = d < ROPE_DIM // 2
    rot = d < ROPE_DIM
    cos = jnp.where(rot, jnp.cos(ang), 1.0)
    sin = jnp.sin(ang)
    sin = jnp.where(first, -sin, jnp.where(rot, sin, 0.0))

    def proj(g):
        return _dot(xb, w_ref[:, g * WIDTH:(g + 1) * WIDTH])

    def rotary(p):
        outs = []
        for j in range(WIDTH // LANES):
            xj = p[:, j * LANES:(j + 1) * LANES]
            partner = jnp.where(first, pltpu.roll(xj, LANES - ROPE_DIM // 2, 1),
                                pltpu.roll(xj, ROPE_DIM // 2, 1))
            outs.append(xj * cos + partner * sin)
        return jnp.concatenate(outs, axis=1)

    scale = HALF_COLS ** -0.5
    sbq[...] = (proj(0) * scale).astype(BF16)
    sbk[...] = proj(1).astype(BF16)
    sbv[...] = proj(2).astype(BF16)
    dfq[...] = (rotary(proj(3)) * scale).astype(BF16)
    dfk[...] = rotary(proj(4)).astype(BF16)
    vt = proj(5).T
    for h in range(WIDTH // HEAD_COLS):
        dfv[0, h, :HEAD_COLS, :] = vt[h * HEAD_COLS:(h + 1) * HEAD_COLS].astype(BF16)
        dfv[0, h, HEAD_COLS:, :] = jnp.ones((DF_ONES_ROWS, vt.shape[1]), BF16)
    mx_q = proj(6).astype(BF16)
    mx_scale = MX_HEAD_DIM ** -0.5
    for h in range(WIDTH // MX_HEAD_DIM):
        cols = slice(h * MX_HEAD_DIM, (h + 1) * MX_HEAD_DIM)
        s = _dot_nt(mx_q[:, cols], mk_ref[:, cols]) * mx_scale
        p = jnp.exp(s - jnp.max(s, axis=1, keepdims=True))
        p = p / jnp.sum(p, axis=1, keepdims=True)
        omx[:, cols] = _dot(p.astype(BF16), mv_ref[:, cols]).astype(BF16)


def _in_proj(x2, pos2, inv_lane, w_in, mem_k, mem_v, tm):
    T, D = x2.shape
    B, M, _ = mem_k.shape
    tiles_per_batch = T // B // tm
    kv_spec = pl.BlockSpec((None, M, WIDTH), lambda i: (i // tiles_per_batch, 0, 0))
    out = jax.ShapeDtypeStruct((T, WIDTH), BF16)
    blk = pl.BlockSpec((tm, WIDTH), lambda i: (i, 0))
    heads = WIDTH // HEAD_COLS
    vt_rows = HEAD_COLS + DF_ONES_ROWS
    vt_out = jax.ShapeDtypeStruct((T // tm, heads, vt_rows, tm), BF16)
    vt_blk = pl.BlockSpec((1, heads, vt_rows, tm), lambda i: (i, 0, 0, 0))
    return pl.pallas_call(
        _in_proj_body,
        grid=(T // tm,),
        in_specs=[pl.BlockSpec((tm, D), lambda i: (i, 0)),
                  pl.BlockSpec((tm, 1), lambda i: (i, 0)),
                  pl.BlockSpec((1, LANES), lambda i: (0, 0)),
                  pl.BlockSpec((D, N_GROUPS * WIDTH), lambda i: (0, 0)), kv_spec, kv_spec],
        out_specs=[blk] * 5 + [vt_blk, blk],
        out_shape=[out] * 5 + [vt_out, out],
        compiler_params=_cparams("parallel"),
        name="in_proj",
    )(x2, pos2, inv_lane, w_in, mem_k, mem_v)


def _sb_body(q_ref, k_ref, v_ref, o_ref, acc_ref, c_ref, *, tq):
    g = pl.program_id(2)
    lane = lax.broadcasted_iota(jnp.int32, (tq, HEAD_COLS), 1)
    low = lane < HALF_COLS
    qs = []
    for tile in range(2):
        qf = q_ref[tile * tq:(tile + 1) * tq, :].astype(F32)
        qs.append((jnp.where(low, qf, 0.0).astype(BF16), jnp.where(low, 0.0, qf).astype(BF16)))
    row = lax.broadcasted_iota(jnp.int32, (tq, tq), 0)
    col = lax.broadcasted_iota(jnp.int32, (tq, tq), 1)
    later = (row > col).astype(BF16)
    acc_ref[...] = jnp.zeros_like(acc_ref)
    c_ref[...] = jnp.zeros_like(c_ref)

    def block(tile, kb, diagonal):
        start = pl.multiple_of(kb * tq, tq)
        k = k_ref[pl.ds(start, tq), :]
        v = v_ref[pl.ds(start, tq), :]
        cmax = None
        for head in range(2):
            h = 2 * tile + head
            z = _dot_nt(qs[tile][head], k)
            t = jnp.log(1.0 + jnp.exp(-jnp.abs(z)))
            log_beta = jnp.minimum(z, 0.0) - t
            lom = -jnp.maximum(z, 0.0) - t
            if diagonal:
                lom = jnp.where(col < row, lom, 0.0)
            hi = lom.astype(BF16)
            lo = (lom - hi.astype(F32)).astype(BF16)
            rest = _dot(hi, later) + _dot(lo, later)
            c = c_ref[h]
            a = jnp.exp(log_beta + rest + c)
            if diagonal:
                a = jnp.where(col < row, a, 0.0)
            acc_ref[h] += _dot(a.astype(BF16), v)
            c_new = c + rest[:, 0:1] + lom[:, 0:1]
            c_ref[h] = c_new
            m = jnp.max(c_new)
            cmax = m if cmax is None else jnp.maximum(cmax, m)
        return cmax

    def cond(carry):
        r, c0, c1 = carry
        more = jnp.logical_or(c0 > SB_SKIP_BELOW, c1 > SB_SKIP_BELOW)
        return jnp.logical_and(2 * g - 1 - r >= 0, more)

    def body(carry):
        r = carry[0]
        kb = 2 * g - 1 - r
        return r + 1, block(0, kb, False), block(1, kb + 1, False)

    r, _, c1 = lax.while_loop(cond, body, (jnp.int32(0), block(0, 2 * g, True), block(1, 2 * g + 1, True)))

    @pl.when(jnp.logical_and(r == 2 * g, c1 > SB_SKIP_BELOW))
    def _():
        block(1, 0, False)

    for tile in range(2):
        o_ref[tile * tq:(tile + 1) * tq, :] = jnp.where(
            low, acc_ref[2 * tile], acc_ref[2 * tile + 1]).astype(BF16)


def _sb_attention(q, k, v, B, S):
    T = q.shape[0]
    tq = _tile(S // 2, 256)
    ng = S // (2 * tq)
    k3 = k.reshape(B, S, WIDTH)
    v3 = v.reshape(B, S, WIDTH)
    kv_spec = pl.BlockSpec((None, S, HEAD_COLS), lambda b, p, g: (b, 0, p))
    q_spec = pl.BlockSpec((2 * tq, HEAD_COLS), lambda b, p, g: (b * ng + g, p))
    return pl.pallas_call(
        functools.partial(_sb_body, tq=tq),
        grid=(B, WIDTH // HEAD_COLS, ng),
        in_specs=[q_spec, kv_spec, kv_spec],
        out_specs=q_spec,
        out_shape=jax.ShapeDtypeStruct((T, WIDTH), BF16),
        scratch_shapes=[pltpu.VMEM((4, tq, HEAD_COLS), F32), pltpu.VMEM((4, tq, 1), F32)],
        compiler_params=_cparams("parallel", "parallel", "arbitrary"),
        name="sb_attention",
    )(q, k3, v3)


def _df_body(lam_ref, g_ref, q_ref, k_ref, vt_ref, o_ref, m_ref, acc_ref, sa_ref, sb_ref, *, tq,
             lambda_init):
    i = pl.program_id(2)
    qt = q_ref[...].astype(F32).T
    sub = lax.broadcasted_iota(jnp.int32, qt.shape, 0)
    low = sub < HALF_COLS
    qts = (jnp.where(low, qt, 0.0).astype(BF16), jnp.where(low, 0.0, qt).astype(BF16))
    m_ref[...] = jnp.full_like(m_ref, -jnp.inf)
    acc_ref[...] = jnp.zeros_like(acc_ref)

    def scores(kb, s_ref):
        k = k_ref[pl.ds(pl.multiple_of(kb * tq, tq), tq), :]
        for m in range(2):
            s_ref[m] = _dot(k, qts[m])

    def consume(kb, s_ref, diagonal):
        vt = vt_ref[kb]
        for m in range(2):
            s = s_ref[m]
            if diagonal:
                key = lax.broadcasted_iota(jnp.int32, (tq, tq), 0)
                qry = lax.broadcasted_iota(jnp.int32, (tq, tq), 1)
                s = jnp.where(key <= qry, s, -jnp.inf)
            m_old = m_ref[m]
            m_new = jnp.maximum(m_old, jnp.max(s, axis=0, keepdims=True))
            p = jnp.exp(s - m_new).astype(BF16)
            acc_ref[m] = jnp.exp(m_old - m_new) * acc_ref[m] + _dot(vt, p)
            m_ref[m] = m_new

    scores(0, sa_ref)

    def pair(j, carry):
        scores(2 * j + 1, sb_ref)
        consume(2 * j, sa_ref, False)
        scores(2 * j + 2, sa_ref)
        consume(2 * j + 1, sb_ref, False)
        return carry

    lax.fori_loop(0, lax.shift_right_logical(i, 1), pair, 0)
    odd = (i & 1) == 1

    @pl.when(odd)
    def _():
        scores(i, sb_ref)
        consume(i - 1, sa_ref, False)
        consume(i, sb_ref, True)

    @pl.when(jnp.logical_not(odd))
    def _():
        consume(i, sa_ref, True)

    lp = lam_ref[...]
    lam = (jnp.exp(jnp.sum(lp[0:1] * lp[1:2], axis=1, keepdims=True))
           - jnp.exp(jnp.sum(lp[2:3] * lp[3:4], axis=1, keepdims=True)) + lambda_init)
    outs = [acc_ref[m, :HEAD_COLS, :] / acc_ref[m, HEAD_COLS:HEAD_COLS + 1, :] for m in range(2)]
    o = (outs[0] - lam * outs[1]).T
    o = o * lax.rsqrt(jnp.mean(jnp.square(o), axis=1, keepdims=True) + RMS_EPS)
    o_ref[...] = (o * g_ref[...] * (1.0 - lambda_init)).astype(BF16)


def _df_attention(q, k, vt, lam_p, norm_g, lambda_init, B, S, tq):
    T = q.shape[0]
    nq = S // tq
    k3 = k.reshape(B, S, WIDTH)
    vt_rows = vt.shape[2]
    return pl.pallas_call(
        functools.partial(_df_body, tq=tq, lambda_init=lambda_init),
        grid=(B, WIDTH // HEAD_COLS, nq),
        in_specs=[pl.BlockSpec(lam_p.shape, lambda b, h, i: (0, 0)),
                  pl.BlockSpec((1, HEAD_COLS), lambda b, h, i: (0, 0)),
                  pl.BlockSpec((tq, HEAD_COLS), lambda b, h, i: (b * nq + i, h)),
                  pl.BlockSpec((None, S, HEAD_COLS), lambda b, h, i: (b, 0, h)),
                  pl.BlockSpec((nq, None, vt_rows, tq), lambda b, h, i: (b, h, 0, 0))],
        out_specs=pl.BlockSpec((tq, HEAD_COLS), lambda b, h, i: (b * nq + i, h)),
        out_shape=jax.ShapeDtypeStruct((T, WIDTH), BF16),
        scratch_shapes=[pltpu.VMEM((2, 1, tq), F32), pltpu.VMEM((2, vt_rows, tq), F32),
                        pltpu.VMEM((2, tq, tq), F32), pltpu.VMEM((2, tq, tq), F32)],
        compiler_params=_cparams("parallel", "parallel", "arbitrary"),
        name="df_attention",
    )(lam_p, norm_g, q, k3, vt)


def _mem_kv_body(mem_ref, w_ref, k_ref, v_ref):
    kv = _dot(mem_ref[...].astype(BF16), w_ref[...])
    k_ref[...] = kv[:, :WIDTH].astype(BF16)
    v_ref[...] = kv[:, WIDTH:].astype(BF16)


def _mem_kv(mem2, w_kv):
    n, D = mem2.shape
    out = jax.ShapeDtypeStruct((n, WIDTH), BF16)
    return pl.pallas_call(
        _mem_kv_body,
        grid=(1,),
        in_specs=[pl.BlockSpec((n, D), lambda i: (0, 0)), pl.BlockSpec(w_kv.shape, lambda i: (0, 0))],
        out_specs=[pl.BlockSpec((n, WIDTH), lambda i: (0, 0))] * 2,
        out_shape=[out, out],
        compiler_params=_cparams("arbitrary"),
        name="mem_kv",
    )(mem2, w_kv)


def _layer_norm(y, g, b):
    mu = jnp.mean(y, axis=1, keepdims=True)
    yc = y - mu
    var = jnp.mean(jnp.square(yc), axis=1, keepdims=True)
    return yc * lax.rsqrt(var + LN_EPS) * g + b


def _merge_body(x_ref, osb_ref, odf_ref, omx_ref, wg_ref, bg_ref, wsb_ref, wdf_ref, wmx_ref, wo_ref,
                g_ref, b_ref, x1_ref, x1t_ref, *, alpha):
    x = x_ref[...]
    xb = x.astype(BF16)
    D = x.shape[1]
    merged = None
    for n, (o_ref, w_ref) in enumerate(((osb_ref, wsb_ref), (odf_ref, wdf_ref), (omx_ref, wmx_ref))):
        cols = slice(n * D, (n + 1) * D)
        gate = 1.0 / (1.0 + jnp.exp(-(_dot(xb, wg_ref[:, cols]) + bg_ref[:, cols])))
        term = gate * _dot(o_ref[...], w_ref[...])
        merged = term if merged is None else merged + term
    mix = _dot(merged.astype(BF16), wo_ref[...])
    x1 = _layer_norm(alpha * x + mix, g_ref[...], b_ref[...])
    x1_ref[...] = x1
    x1t_ref[...] = x1.T.astype(BF16)


def _merge(x2, o_sb, o_df, o_mx, w_gate, b_gate, w_sb, w_df, w_mx, w_out, g, b, alpha):
    T, D = x2.shape
    tm = _tile(T, 512)
    full = lambda a: pl.BlockSpec(a.shape, lambda i: (0, 0))
    tok = lambda w: pl.BlockSpec((tm, w), lambda i: (i, 0))
    return pl.pallas_call(
        functools.partial(_merge_body, alpha=alpha),
        grid=(T // tm,),
        in_specs=[tok(D), tok(WIDTH), tok(WIDTH), tok(WIDTH), full(w_gate), full(b_gate), full(w_sb),
                  full(w_df), full(w_mx), full(w_out), full(g), full(b)],
        out_specs=[tok(D), pl.BlockSpec((D, tm), lambda i: (0, i))],
        out_shape=[jax.ShapeDtypeStruct((T, D), F32), jax.ShapeDtypeStruct((D, T), BF16)],
        compiler_params=_cparams("parallel"),
        name="merge_ln",
    )(x2, o_sb, o_df, o_mx, w_gate, b_gate, w_sb, w_df, w_mx, w_out, g, b)


def _top16(s, tie_order):
    idx = lax.broadcasted_iota(jnp.int32, s.shape, 0).astype(F32)
    work = s
    rank = jnp.full(s.shape, float(PEER_TOPK), F32)
    vals = []
    for r in range(PEER_TOPK):
        m = jnp.max(work, axis=0, keepdims=True)
        if tie_order:
            first = jnp.min(jnp.where(work == m, idx, float(s.shape[0])), axis=0, keepdims=True)
            hit = idx == first
        else:
            hit = work == m
        rank = jnp.where(hit, float(r), rank)
        work = jnp.where(hit, -jnp.inf, work)
        vals.append(m)
    return rank, vals


def _pair_select(v0, v1, tie_order):
    K = PEER_TOPK
    G = K // 2
    t1 = jnp.concatenate(v1, axis=0)
    t0_tail = jnp.concatenate(v0[G:], axis=0)
    sub = lax.broadcasted_iota(jnp.int32, (G,) + t1.shape[1:], 0)
    subf = sub.astype(F32)
    vals = [v0[0] + t1[:G], v0[0] + t1[G:]]
    pos = [subf, subf + float(G)]
    for a in range(1, G):
        vals.append(jnp.where(sub < K // (a + 1), v0[a] + t1[:G], -jnp.inf))
        pos.append(subf + float(a * K))
    vals.append(t0_tail + v1[0])
    pos.append((subf + float(G)) * float(K))
    cand = jnp.concatenate(vals, axis=0)
    posc = jnp.concatenate(pos, axis=0)
    work = cand
    sel = jnp.zeros(cand.shape, F32)
    for _ in range(K):
        m = jnp.max(work, axis=0, keepdims=True)
        if tie_order:
            first = jnp.min(jnp.where(work == m, posc, float(K * K)), axis=0, keepdims=True)
            hit = posc == first
        else:
            hit = work == m
        sel = jnp.where(hit, 1.0, sel)
        work = jnp.where(hit, -jnp.inf, work)
    rows = [jnp.sum(sel[:K], axis=0, keepdims=True)]
    rows += [jnp.sum(sel[K + G * (a - 1):K + G * a], axis=0, keepdims=True) for a in range(1, G)]
    nb = jnp.concatenate(rows + [sel[K + G * (G - 1):]], axis=0)
    z = jnp.sum(jnp.where(sel > 0.0, jnp.exp(cand - (v0[0] + v1[0])), 0.0), axis=0, keepdims=True)
    return nb, z, jnp.sum(sel, axis=0, keepdims=True)


def _route_body(xt_ref, wq_ref, keys_ref, a_ref, nbi_ref, bw_ref, r1_ref, q_scr):
    q_scr[...] = _dot(wq_ref[...], xt_ref[...]).astype(BF16)

    def head(h, carry):
        scores = []
        for p in range(2):
            hp = h * 2 + p
            q_hp = q_scr[pl.ds(pl.multiple_of(hp * PEER_HALF, PEER_HALF), PEER_HALF), :]
            scores.append(_dot(keys_ref[hp], q_hp))

        def select(tie_order):
            rank0, v0 = _top16(scores[0], tie_order)
            rank1, v1 = _top16(scores[1], tie_order)
            nb, z, n_pairs = _pair_select(v0, v1, tie_order)
            nbi = jnp.zeros_like(rank0)
            for a in range(PEER_TOPK):
                nbi = jnp.where(rank0 == float(a), nb[a:a + 1, :], nbi)
            keep0 = rank0 < float(PEER_TOPK)
            keep1 = rank1 < float(PEER_TOPK)
            a_ref[h] = jnp.where(keep0, 0.5 * jnp.exp(scores[0] - v0[0]), 0.0)
            nbi_ref[h] = nbi
            bw_ref[h] = jnp.where(keep1, jnp.exp(scores[1] - v1[0]) / z, 0.0).astype(BF16)
            r1_ref[h] = rank1.astype(BF16)
            n0 = jnp.sum(keep0.astype(F32), axis=0, keepdims=True)
            n1 = jnp.sum(keep1.astype(F32), axis=0, keepdims=True)
            k = float(PEER_TOPK)
            return jnp.max(jnp.abs(n0 - k) + jnp.abs(n1 - k) + jnp.abs(n_pairs - k))

        miscount = select(False)

        @pl.when(miscount > 0.0)
        def _():
            select(True)

        return carry

    lax.fori_loop(0, PEER_HEADS, head, 0)


def _route(x1t, wq_t, keys):
    D, T = x1t.shape
    tm = _tile(T, 256)
    out = jax.ShapeDtypeStruct((PEER_HEADS, PEER_NKEYS, T), F32)
    out_b = jax.ShapeDtypeStruct((PEER_HEADS, PEER_NKEYS, T), BF16)
    blk = pl.BlockSpec((PEER_HEADS, PEER_NKEYS, tm), lambda i: (0, 0, i))
    return pl.pallas_call(
        _route_body,
        grid=(T // tm,),
        in_specs=[pl.BlockSpec((D, tm), lambda i: (0, i)),
                  pl.BlockSpec(wq_t.shape, lambda i: (0, 0)),
                  pl.BlockSpec(keys.shape, lambda i: (0, 0, 0))],
        out_specs=[blk] * 4,
        out_shape=[out, out, out_b, out_b],
        scratch_shapes=[pltpu.VMEM((wq_t.shape[0], tm), BF16)],
        compiler_params=_cparams("parallel"),
        name="peer_route",
    )(x1t, wq_t, keys)


def _tables_body(u_ref, v_ref, ub_ref, vt_ref):
    ub_ref[...] = u_ref[...].astype(BF16)
    vt_ref[...] = v_ref[...].T.astype(BF16)


def _tables(u_tab, v_tab):
    E, D = u_tab.shape
    te = _tile(E, 512)
    return pl.pallas_call(
        _tables_body,
        grid=(E // te,),
        in_specs=[pl.BlockSpec((te, D), lambda e: (e, 0))] * 2,
        out_specs=[pl.BlockSpec((te, D), lambda e: (e, 0)), pl.BlockSpec((D, te), lambda e: (0, e))],
        out_shape=[jax.ShapeDtypeStruct((E, D), BF16), jax.ShapeDtypeStruct((D, E), BF16)],
        compiler_params=_cparams("parallel"),
        name="peer_tables",
    )(u_tab, v_tab)


def _token_row_bf16(ref, hd, i):
    tile = jnp.broadcast_to(ref[hd, i:i + 1, :], (BF16_SUBLANES, ref.shape[2])).astype(BF16)
    return jnp.concatenate([tile] * (PEER_NKEYS // BF16_SUBLANES), axis=0)


def _experts_body(xt_ref, u_ref, vt_ref, a_ref, nbi_ref, bw_ref, r1_ref, x1_ref, g_ref, b_ref, o_ref,
                  acc_ref, *, alpha, rows_per_step):
    e = pl.program_id(1)

    @pl.when(e == 0)
    def _():
        acc_ref[...] = jnp.zeros_like(acc_ref)

    xt = xt_ref[...]
    acts = []
    chunk = EXPERT_CHUNK
    for c in range(rows_per_step * PEER_NKEYS // chunk):
        hid_c = _dot(u_ref[c * chunk:(c + 1) * chunk, :], xt)
        for r in range(chunk // PEER_NKEYS):
            i = c * (chunk // PEER_NKEYS) + r
            hid = hid_c[r * PEER_NKEYS:(r + 1) * PEER_NKEYS]
            gate = None
            for hd in range(PEER_HEADS):
                a = _token_row_bf16(a_ref, hd, i)
                nbi = _token_row_bf16(nbi_ref, hd, i)
                term = a * jnp.where(r1_ref[hd] < nbi, bw_ref[hd], jnp.zeros((), BF16))
                gate = term if gate is None else gate + term
            gelu2 = hid * (1.0 + lax.erf(hid * (2.0 ** -0.5)))
            acts.append(gelu2.astype(BF16) * gate)
    acc_ref[...] += _dot(vt_ref[...], jnp.concatenate(acts, axis=0))

    @pl.when(e == pl.num_programs(1) - 1)
    def _():
        y = alpha * x1_ref[...] + acc_ref[...].T
        o_ref[...] = _layer_norm(y, g_ref[...], b_ref[...])


def _experts(x1, x1t, u_b, v_t, a, nbi, bw, r1, g, b, alpha):
    T, D = x1.shape
    E = u_b.shape[0]
    tm = _tile(T, 512)
    te = _tile(E, 2048)
    route_spec = pl.BlockSpec((PEER_HEADS, PEER_NKEYS, tm), lambda t, e: (0, 0, t))
    row_spec = pl.BlockSpec((PEER_HEADS, te // PEER_NKEYS, tm), lambda t, e: (0, e, t))
    vec = lambda arr: pl.BlockSpec(arr.shape, lambda t, e: (0, 0))
    return pl.pallas_call(
        functools.partial(_experts_body, alpha=alpha, rows_per_step=te // PEER_NKEYS),
        grid=(T // tm, E // te),
        in_specs=[pl.BlockSpec((D, tm), lambda t, e: (0, t)),
                  pl.BlockSpec((te, D), lambda t, e: (e, 0)),
                  pl.BlockSpec((D, te), lambda t, e: (0, e)),
                  row_spec, row_spec, route_spec, route_spec,
                  pl.BlockSpec((tm, D), lambda t, e: (t, 0)), vec(g), vec(b)],
        out_specs=pl.BlockSpec((tm, D), lambda t, e: (t, 0)),
        out_shape=jax.ShapeDtypeStruct((T, D), F32),
        scratch_shapes=[pltpu.VMEM((D, tm), F32)],
        compiler_params=_cparams("parallel", "arbitrary"),
        name="peer_experts",
    )(x1t, u_b, v_t, a, nbi, bw, r1, x1, g, b)


def kernel(x, mem, positions, w_in, w_mem_kv, df_lambda, df_norm_g, w_gate, b_gate, w_br_sb, w_br_df,
           w_br_mx, w_out, ln1_g, ln1_b, w_query, sub_keys, u_tab, v_tab, ln2_g, ln2_b):
    B, S, D = x.shape
    M = mem.shape[1]
    depth = w_in.shape[0]
    T = B * S
    alpha = (2.0 * depth) ** 0.25

    inv = ROPE_THETA ** (-jnp.arange(0, ROPE_DIM, 2, dtype=F32) / ROPE_DIM)
    d = jnp.arange(LANES) % HALF_COLS
    inv_lane = jnp.where(d < ROPE_DIM, inv[d % (ROPE_DIM // 2)], 0.0).reshape(1, LANES).astype(F32)
    pos2 = positions.reshape(T, 1)
    mem2 = mem.reshape(B * M, D)
    x2 = x.reshape(T, D)
    row = lambda v: v.reshape(1, -1)

    for l in range(depth):
        lambda_init = 0.8 - 0.6 * math.exp(-0.3 * l)
        t_df = _tile(S, 512)
        mem_k, mem_v = _mem_kv(mem2, w_mem_kv[l].astype(BF16))
        sbq, sbk, sbv, dfq, dfk, dfvt, o_mx = _in_proj(
            x2, pos2, inv_lane, w_in[l].astype(BF16), mem_k.reshape(B, M, WIDTH),
            mem_v.reshape(B, M, WIDTH), t_df)
        o_sb = _sb_attention(sbq, sbk, sbv, B, S)
        o_df = _df_attention(dfq, dfk, dfvt, df_lambda[l], row(df_norm_g[l]), lambda_init, B, S, t_df)
        x1, x1t = _merge(x2, o_sb, o_df, o_mx, w_gate[l].astype(BF16), row(b_gate[l]),
                         w_br_sb[l].astype(BF16), w_br_df[l].astype(BF16), w_br_mx[l].astype(BF16),
                         w_out[l].astype(BF16), row(ln1_g[l]), row(ln1_b[l]), alpha)
        keys = sub_keys[l].reshape(PEER_HEADS * 2, PEER_NKEYS, PEER_HALF).astype(BF16)
        a, nbi, bw, r1 = _route(x1t, w_query[l].T.astype(BF16), keys)
        u_b, v_t = _tables(u_tab[l], v_tab[l])
        x2 = _experts(x1, x1t, u_b, v_t, a, nbi, bw, r1, row(ln2_g[l]), row(ln2_b[l]), alpha)
    return x2.reshape(B, S, D)
```
